```python
import math
import jax, jax.numpy as jnp
from jax import lax
import numpy as np

D_MODEL = 2048
BATCH = 2
SEQ = 4096
DEPTH = 2
DEC_BATCH = 2
DEC_SEQ = 16384
PAST_LEN = 128

CONV_CH = D_MODEL // 4
CONV_WIDTH = 3
ATT_HD = 64
ATT_VD = 2 * ATT_HD
ATT_WIDTH = D_MODEL // 2
ATT_HEADS = ATT_WIDTH // ATT_VD
ATT_QK = ATT_HEADS * 2 * ATT_HD
ROT_DIM = ATT_HD // 4
ROPE_THETA = 500000.0
Q_BLOCK = 128
MLP_WIDTH = D_MODEL // 4
CHUNK = 128
MLP_HD = 128
MLP_HEADS = MLP_WIDTH // MLP_HD
D_FF = 4 * D_MODEL
EPS = 1e-6
SPLIT_SIZES = (CONV_CH, CONV_CH, CONV_CH, ATT_QK, ATT_QK, ATT_WIDTH, MLP_WIDTH, MLP_WIDTH)
IN_WIDTH = 3 * CONV_CH + 2 * ATT_QK + ATT_WIDTH + 2 * MLP_WIDTH
MIX_WIDTH = CONV_CH + ATT_WIDTH + MLP_WIDTH

kernel_name = "hymba_conv_diffattn_gmlp_encoder"


def rms_norm(x, g):
    xf = x.astype(jnp.float32)
    y = xf * lax.rsqrt(jnp.mean(xf * xf, axis=-1, keepdims=True) + EPS)
    return (y * g.astype(jnp.float32)).astype(x.dtype)


def rope_tables(s):
    inv = ROPE_THETA ** (-jnp.arange(0, ROT_DIM, 2, dtype=jnp.float32) / ROT_DIM)
    ang = jnp.arange(s, dtype=jnp.float32)[:, None] * inv[None, :]
    return jnp.cos(ang), jnp.sin(ang)


def apply_partial_rope(x, cos, sin):
    half = ROT_DIM // 2
    xf = x.astype(jnp.float32)
    x1 = xf[..., :half]
    x2 = xf[..., half:ROT_DIM]
    c = cos[None, :, None, None, :]
    s = sin[None, :, None, None, :]
    out = jnp.concatenate([x1 * c - x2 * s, x2 * c + x1 * s, xf[..., ROT_DIM:]], axis=-1)
    return out.astype(x.dtype)


def short_conv_mixer(xa, gb, gc, w_conv):
    z = gc * xa
    zp = jnp.pad(z, ((0, 0), (1, 1), (0, 0)))
    conv = w_conv[0] * zp[:, :-2] + w_conv[1] * zp[:, 1:-1] + w_conv[2] * zp[:, 2:]
    return gb * conv


def diff_attention(q, k, v, lam, g_subln, lambda_init):
    b, s = q.shape[0], q.shape[1]
    nb = s // Q_BLOCK
    qb = q.reshape(b, nb, Q_BLOCK, ATT_HEADS, 2, ATT_HD).swapaxes(0, 1)
    scale = ATT_HD ** -0.5

    def block(qblk):
        sc = jnp.einsum('bqhcd,bkhcd->bhcqk', qblk, k).astype(jnp.float32) * scale
        p = jax.nn.softmax(sc, axis=-1)
        a = p[:, :, 0] - lam * p[:, :, 1]
        return jnp.einsum('bhqk,bkhe->bqhe', a.astype(v.dtype), v)

    o = lax.map(block, qb)
    o = o.swapaxes(0, 1).reshape(b, s, ATT_HEADS, ATT_VD)
    o = rms_norm(o, g_subln) * (1.0 - lambda_init)
    return o.reshape(b, s, ATT_WIDTH)


def chunk_spatial_gating(u, v, w_s, b_s, g_v):
    b, s, _ = u.shape
    n = s // CHUNK
    vn = rms_norm(v.reshape(b, n, CHUNK, MLP_HEADS, MLP_HD), g_v)
    mixed = jnp.einsum('hqp,bnphd->bnqhd', w_s, vn) + b_s.T[None, None, :, :, None]
    return u * mixed.reshape(b, s, MLP_WIDTH)


def encoder_layer(x, l, cos, sin, norm1_g, w_in, conv_w, q_norm_g, k_norm_g,
                  lam_q1, lam_k1, lam_q2, lam_k2, subln_g, sgu_norm_g, sgu_w, sgu_b,
                  w_out, norm2_g, w_up, w_down):
    b, s, _ = x.shape
    h = rms_norm(x, norm1_g[l])
    proj = h @ w_in[l]
    points = []
    acc = 0
    for sz in SPLIT_SIZES[:-1]:
        acc += sz
        points.append(acc)
    a_x, a_b, a_c, q, k, v, c_u, c_v = jnp.split(proj, points, axis=-1)

    out_a = short_conv_mixer(a_x, a_b, a_c, conv_w[l])

    q = q.reshape(b, s, ATT_HEADS, 2, ATT_HD)
    k = k.reshape(b, s, ATT_HEADS, 2, ATT_HD)
    v = v.reshape(b, s, ATT_HEADS, ATT_VD)
    q = apply_partial_rope(rms_norm(q, q_norm_g[l]), cos, sin)
    k = apply_partial_rope(rms_norm(k, k_norm_g[l]), cos, sin)
    lambda_init = 0.8 - 0.6 * math.exp(-0.3 * l)
    lam = (jnp.exp(jnp.sum(lam_q1[l].astype(jnp.float32) * lam_k1[l].astype(jnp.float32)))
           - jnp.exp(jnp.sum(lam_q2[l].astype(jnp.float32) * lam_k2[l].astype(jnp.float32)))
           + lambda_init)
    out_b = diff_attention(q, k, v, lam, subln_g[l], lambda_init)

    out_c = chunk_spatial_gating(jax.nn.gelu(c_u), jax.nn.gelu(c_v), sgu_w[l], sgu_b[l], sgu_norm_g[l])

    mixed = jnp.concatenate([out_a, out_b, out_c], axis=-1)
    x = x + mixed @ w_out[l]

    h2 = rms_norm(x, norm2_g[l])
    x = x + jnp.square(jax.nn.relu(h2 @ w_up[l])) @ w_down[l]
    return x


def run_trunk(x, norm1_g, w_in, conv_w, q_norm_g, k_norm_g, lam_q1, lam_k1, lam_q2, lam_k2,
              subln_g, sgu_norm_g, sgu_w, sgu_b, w_out, norm2_g, w_up, w_down):
    cos, sin = rope_tables(x.shape[1])
    for l in range(DEPTH):
        x = encoder_layer(x, l, cos, sin, norm1_g, w_in, conv_w, q_norm_g, k_norm_g,
                          lam_q1, lam_k1, lam_q2, lam_k2, subln_g, sgu_norm_g, sgu_w, sgu_b,
                          w_out, norm2_g, w_up, w_down)
    return x


def setup_inputs(seed: int = 0) -> dict:
    key = jax.random.key(seed)
    ks = jax.random.split(key, 20)
    f32 = jnp.float32

    def nrm(k, shape, scale):
        return jax.random.normal(k, shape, f32) * scale

    def gain(k, shape):
        return 1.0 + 0.02 * jax.random.normal(k, shape, f32)

    return {
        "x_prompt": nrm(ks[0], (BATCH, SEQ, D_MODEL), 1.0),
        "x_sample": nrm(ks[1], (DEC_BATCH, DEC_SEQ, D_MODEL), 1.0),
        "norm1_g": gain(ks[2], (DEPTH, D_MODEL)),
        "w_in": nrm(ks[3], (DEPTH, D_MODEL, IN_WIDTH), D_MODEL ** -0.5),
        "conv_w": nrm(ks[4], (DEPTH, CONV_WIDTH, CONV_CH), CONV_WIDTH ** -0.5),
        "q_norm_g": gain(ks[5], (DEPTH, ATT_HD)),
        "k_norm_g": gain(ks[6], (DEPTH, ATT_HD)),
        "lam_q1": nrm(ks[7], (DEPTH, ATT_HD), 0.1),
        "lam_k1": nrm(ks[8], (DEPTH, ATT_HD), 0.1),
        "lam_q2": nrm(ks[9], (DEPTH, ATT_HD), 0.1),
        "lam_k2": nrm(ks[10], (DEPTH, ATT_HD), 0.1),
        "subln_g": gain(ks[11], (DEPTH, ATT_VD)),
        "sgu_norm_g": gain(ks[12], (DEPTH, MLP_HD)),
        "sgu_w": nrm(ks[13], (DEPTH, MLP_HEADS, CHUNK, CHUNK), CHUNK ** -0.5),
        "sgu_b": 1.0 + nrm(ks[14], (DEPTH, MLP_HEADS, CHUNK), 0.01),
        "w_out": nrm(ks[15], (DEPTH, MIX_WIDTH, D_MODEL), MIX_WIDTH ** -0.5),
        "norm2_g": gain(ks[16], (DEPTH, D_MODEL)),
        "w_up": nrm(ks[17], (DEPTH, D_MODEL, D_FF), D_MODEL ** -0.5),
        "w_down": nrm(ks[18], (DEPTH, D_FF, D_MODEL), D_FF ** -0.5),
    }


def reference(x_prompt, x_sample, norm1_g, w_in, conv_w, q_norm_g, k_norm_g, lam_q1, lam_k1,
              lam_q2, lam_k2, subln_g, sgu_norm_g, sgu_w, sgu_b, w_out, norm2_g, w_up, w_down):
    y_prompt = run_trunk(x_prompt, norm1_g, w_in, conv_w, q_norm_g, k_norm_g, lam_q1, lam_k1,
                         lam_q2, lam_k2, subln_g, sgu_norm_g, sgu_w, sgu_b, w_out, norm2_g,
                         w_up, w_down)
    y_sample = run_trunk(x_sample, norm1_g, w_in, conv_w, q_norm_g, k_norm_g, lam_q1, lam_k1,
                         lam_q2, lam_k2, subln_g, sgu_norm_g, sgu_w, sgu_b, w_out, norm2_g,
                         w_up, w_down)
    return (y_prompt, y_sample)
```

```python
import functools
import math

import jax
import jax.numpy as jnp
from jax import lax
from jax.experimental import pallas as pl
from jax.experimental.pallas import tpu as pltpu

F32 = jnp.float32
BF16 = jnp.bfloat16

EPS = 1e-6
ROPE_THETA = 500000.0
CONV_WIDTH = 3
ATT_HD = 64
ATT_VD = 2 * ATT_HD
CHUNK = 128
MLP_HD = 128
LANES = 128
BF16_SUBLANES = 16
COL = 512
VMEM_LIMIT = 56 * 1024 * 1024
LOG2E = math.log2(math.e)


def _rms(x, gain):
    ms = jnp.mean(x * x, axis=-1, keepdims=True)
    return x * lax.rsqrt(ms + EPS) * gain


def _in_proj_kernel(x_ref, g1_ref, w_ref, cos_ref, sa_ref, sb_ref, gq_ref, gk_ref, seg_ref,
                    gv_ref, ws_ref, bs_ref, o_ref, h_sc, u_sc, *, n_conv, n_qk, n_v):
    j = pl.program_id(1)
    tm = x_ref.shape[0]

    @pl.when(j == 0)
    def _():
        h_sc[...] = _rms(x_ref[...], g1_ref[...]).astype(BF16)

    acc = jnp.dot(h_sc[...], w_ref[...], preferred_element_type=F32)

    q_lo, k_lo = n_conv, n_conv + n_qk
    v_lo, u_at = k_lo + n_qk, k_lo + n_qk + n_v

    def qk_norm_rope(gain):
        ms = jnp.dot((acc * acc).astype(BF16), seg_ref[...], preferred_element_type=F32)
        y = acc * lax.rsqrt(ms + EPS) * gain
        cos, sa, sb = cos_ref[...], sa_ref[...], sb_ref[...]
        for c in range(COL // LANES):
            yc = y[:, c * LANES:(c + 1) * LANES]
            r = yc * cos + pltpu.roll(yc, LANES - 8, 1) * sa + pltpu.roll(yc, 8, 1) * sb
            o_ref[:, c * LANES:(c + 1) * LANES] = r.astype(BF16)

    @pl.when((j < q_lo) | ((j >= v_lo) & (j < u_at)))
    def _():
        o_ref[...] = acc.astype(BF16)

    @pl.when((j >= q_lo) & (j < k_lo))
    def _():
        qk_norm_rope(gq_ref[...] * (ATT_HD ** -0.5 * LOG2E))

    @pl.when((j >= k_lo) & (j < v_lo))
    def _():
        qk_norm_rope(gk_ref[...])

    @pl.when(j == u_at)
    def _():
        u_sc[...] = jax.nn.gelu(acc)

    @pl.when(j == u_at + 1)
    def _():
        v = jax.nn.gelu(acc)
        for h in range(COL // MLP_HD):
            sl = slice(h * MLP_HD, (h + 1) * MLP_HD)
            vn = _rms(v[:, sl], gv_ref[:, sl]).astype(BF16)
            w = ws_ref[h]
            for n in range(tm // CHUNK):
                rows = slice(n * CHUNK, (n + 1) * CHUNK)
                mixed = jnp.dot(w, vn[rows, :], preferred_element_type=F32) + bs_ref[:, sl]
                o_ref[rows, sl] = (u_sc[rows, sl] * mixed).astype(BF16)


def _in_proj(x, g1, w_in, tabs, gq, gk, seg, gv, ws, bs, *, seq, tm):
    t, d = x.shape
    n_in = w_in.shape[1] // COL
    n_out = n_in - 1
    pos_blocks = seq // tm
    const = lambda i, j: (0, 0)
    kern = functools.partial(_in_proj_kernel, n_conv=3, n_qk=2, n_v=2)
    return pl.pallas_call(
        kern,
        grid=(t // tm, n_in),
        in_specs=[
            pl.BlockSpec((tm, d), lambda i, j: (i, 0)),
            pl.BlockSpec((1, d), const),
            pl.BlockSpec((d, COL), lambda i, j: (0, j)),
            pl.BlockSpec((tm, LANES), lambda i, j: (i % pos_blocks, 0)),
            pl.BlockSpec((tm, LANES), lambda i, j: (i % pos_blocks, 0)),
            pl.BlockSpec((tm, LANES), lambda i, j: (i % pos_blocks, 0)),
            pl.BlockSpec((1, COL), const),
            pl.BlockSpec((1, COL), const),
            pl.BlockSpec((COL, COL), const),
            pl.BlockSpec((1, COL), const),
            pl.BlockSpec(ws.shape, lambda i, j: (0, 0, 0)),
            pl.BlockSpec((CHUNK, COL), const),
        ],
        out_specs=pl.BlockSpec((tm, COL), lambda i, j: (i, jnp.minimum(j, n_out - 1))),
        out_shape=jax.ShapeDtypeStruct((t, n_out * COL), BF16),
        scratch_shapes=[pltpu.VMEM((tm, d), BF16), pltpu.VMEM((tm, COL), F32)],
        compiler_params=pltpu.CompilerParams(
            dimension_semantics=("parallel", "arbitrary"), vmem_limit_bytes=VMEM_LIMIT),
        name="in_proj",
    )(x, g1, w_in, *tabs, gq, gk, seg, gv, ws, bs)


def _attn_kernel(lamv_ref, q_ref, k_ref, v_ref, g_ref, o_ref, qs_sc, m_sc, l_sc, acc_sc,
                 *, tk, lambda_init):
    tq = q_ref.shape[0]
    seq = k_ref.shape[0]

    q = q_ref[...]
    lane = lax.broadcasted_iota(jnp.int32, q.shape, 1)
    zero = jnp.zeros_like(q)
    qs_sc[0:tq, :] = jnp.where(lane < ATT_HD, q, zero)
    qs_sc[tq:2 * tq, :] = jnp.where(lane >= ATT_HD, q, zero)
    m_sc[...] = jnp.full(m_sc.shape, -jnp.inf, F32)
    l_sc[...] = jnp.zeros(l_sc.shape, F32)
    acc_sc[...] = jnp.zeros(acc_sc.shape, F32)

    def body(j, carry):
        off = pl.multiple_of(j * tk, tk)
        kj = k_ref[pl.ds(off, tk), :]
        vj = v_ref[pl.ds(off, tk), :]
        s = lax.dot_general(qs_sc[...], kj, (((1,), (1,)), ((), ())),
                            preferred_element_type=F32)
        m_prev = m_sc[...]
        m_new = jnp.maximum(m_prev, jnp.max(s, axis=1, keepdims=True))
        alpha = jnp.exp2(m_prev - m_new)
        p = jnp.exp2(s - m_new)
        l_sc[...] = alpha * l_sc[...] + jnp.sum(p, axis=1, keepdims=True)
        acc_sc[...] = alpha * acc_sc[...] + jnp.dot(p.astype(BF16), vj, preferred_element_type=F32)
        m_sc[...] = m_new
        return carry

    lax.fori_loop(0, seq // tk, body, 0)

    lv = lamv_ref[...]
    lam = (jnp.exp(jnp.sum(lv[0:1] * lv[1:2], axis=1, keepdims=True))
           - jnp.exp(jnp.sum(lv[2:3] * lv[3:4], axis=1, keepdims=True)) + lambda_init)
    o_all = acc_sc[...] / l_sc[...]
    o = o_all[0:tq] - lam * o_all[tq:2 * tq]
    o_ref[...] = (_rms(o, g_ref[...]) * (1.0 - lambda_init)).astype(BF16)


def _attention(proj, lamv, g_sub, *, batch, seq, heads, q_col, k_col, v_col, lambda_init, tq, tk):
    nq = seq // tq
    kern = functools.partial(_attn_kernel, tk=tk, lambda_init=lambda_init)
    return pl.pallas_call(
        kern,
        grid=(batch, heads, nq),
        in_specs=[
            pl.BlockSpec(lamv.shape, lambda b, h, i: (0, 0)),
            pl.BlockSpec((tq, ATT_VD), lambda b, h, i: (b * nq + i, q_col + h)),
            pl.BlockSpec((seq, ATT_VD), lambda b, h, i: (b, k_col + h)),
            pl.BlockSpec((seq, ATT_VD), lambda b, h, i: (b, v_col + h)),
            pl.BlockSpec((1, ATT_VD), lambda b, h, i: (0, 0)),
        ],
        out_specs=pl.BlockSpec((tq, ATT_VD), lambda b, h, i: (b * nq + i, h)),
        out_shape=jax.ShapeDtypeStruct((batch * seq, heads * ATT_VD), BF16),
        scratch_shapes=[
            pltpu.VMEM((2 * tq, ATT_VD), BF16),
            pltpu.VMEM((2 * tq, 1), F32),
            pltpu.VMEM((2 * tq, 1), F32),
            pltpu.VMEM((2 * tq, ATT_VD), F32),
        ],
        compiler_params=pltpu.CompilerParams(
            dimension_semantics=("parallel", "parallel", "arbitrary"), vmem_limit_bytes=VMEM_LIMIT),
        name="diff_attention",
    )(lamv, proj, proj, proj, g_sub)


def _out_proj_kernel(ax_ref, ab_ref, ac_ref, axp_ref, acp_ref, axn_ref, acn_ref, cw_ref,
                     att_ref, oc_ref, x_ref, w_ref, o_ref, mix_sc, *, pos_blocks):
    i = pl.program_id(0)
    j = pl.program_id(1)
    tm, cc = ax_ref.shape

    @pl.when(j == 0)
    def _():
        z = ac_ref[...].astype(F32) * ax_ref[...].astype(F32)
        last = BF16_SUBLANES - 1
        zp = acp_ref[last:last + 1, :].astype(F32) * axp_ref[last:last + 1, :].astype(F32)
        zn = acn_ref[0:1, :].astype(F32) * axn_ref[0:1, :].astype(F32)
        zp = jnp.where(i % pos_blocks == 0, jnp.zeros_like(zp), zp)
        zn = jnp.where(i % pos_blocks == pos_blocks - 1, jnp.zeros_like(zn), zn)
        row = lax.broadcasted_iota(jnp.int32, z.shape, 0)
        z_prev = jnp.where(row == 0, zp, pltpu.roll(z, 1, 0))
        z_next = jnp.where(row == tm - 1, zn, pltpu.roll(z, tm - 1, 0))
        cw = cw_ref[...]
        conv = cw[0:1] * z_prev + cw[1:2] * z + cw[2:3] * z_next
        mix_sc[:, 0:cc] = (ab_ref[...].astype(F32) * conv).astype(BF16)
        aw = att_ref.shape[1]
        mix_sc[:, cc:cc + aw] = att_ref[...]
        mix_sc[:, cc + aw:] = oc_ref[...]

    o_ref[...] = x_ref[...] + jnp.dot(mix_sc[...], w_ref[...], preferred_element_type=F32)


def _out_proj(proj, att, x, conv_w, w_out, *, seq, tm, tn, oc_col):
    t, d = x.shape
    pos_blocks = seq // tm
    hb = tm // BF16_SUBLANES
    n_hb = t // BF16_SUBLANES
    prev = lambda i: jnp.maximum(i * hb - 1, 0)
    nxt = lambda i: jnp.minimum((i + 1) * hb, n_hb - 1)
    kern = functools.partial(_out_proj_kernel, pos_blocks=pos_blocks)
    return pl.pallas_call(
        kern,
        grid=(t // tm, d // tn),
        in_specs=[
            pl.BlockSpec((tm, COL), lambda i, j: (i, 0)),
            pl.BlockSpec((tm, COL), lambda i, j: (i, 1)),
            pl.BlockSpec((tm, COL), lambda i, j: (i, 2)),
            pl.BlockSpec((BF16_SUBLANES, COL), lambda i, j: (prev(i), 0)),
            pl.BlockSpec((BF16_SUBLANES, COL), lambda i, j: (prev(i), 2)),
            pl.BlockSpec((BF16_SUBLANES, COL), lambda i, j: (nxt(i), 0)),
            pl.BlockSpec((BF16_SUBLANES, COL), lambda i, j: (nxt(i), 2)),
            pl.BlockSpec(conv_w.shape, lambda i, j: (0, 0)),
            pl.BlockSpec((tm, att.shape[1]), lambda i, j: (i, 0)),
            pl.BlockSpec((tm, COL), lambda i, j: (i, oc_col)),
            pl.BlockSpec((tm, tn), lambda i, j: (i, j)),
            pl.BlockSpec((w_out.shape[0], tn), lambda i, j: (0, j)),
        ],
        out_specs=pl.BlockSpec((tm, tn), lambda i, j: (i, j)),
        out_shape=jax.ShapeDtypeStruct((t, d), F32),
        scratch_shapes=[pltpu.VMEM((tm, w_out.shape[0]), BF16)],
        compiler_params=pltpu.CompilerParams(
            dimension_semantics=("parallel", "arbitrary"), vmem_limit_bytes=VMEM_LIMIT),
        name="out_proj",
    )(proj, proj, proj, proj, proj, proj, proj, conv_w, att, proj, x, w_out)


def _ffn_kernel(x_ref, g_ref, wu_ref, wd_ref, o_ref, h_sc):
    f = pl.program_id(1)

    @pl.when(f == 0)
    def _():
        x = x_ref[...]
        h_sc[...] = _rms(x, g_ref[...]).astype(BF16)
        o_ref[...] = x

    hid = jnp.dot(h_sc[...], wu_ref[...], preferred_element_type=F32)
    hid = jnp.square(jnp.maximum(hid, 0.0)).astype(BF16)
    o_ref[...] += jnp.dot(hid, wd_ref[...], preferred_element_type=F32)


def _ffn(x, g2, w_up, w_down, *, tm, tf):
    t, d = x.shape
    ff = w_up.shape[1]
    return pl.pallas_call(
        _ffn_kernel,
        grid=(t // tm, ff // tf),
        in_specs=[
            pl.BlockSpec((tm, d), lambda i, f: (i, 0)),
            pl.BlockSpec((1, d), lambda i, f: (0, 0)),
            pl.BlockSpec((d, tf), lambda i, f: (0, f)),
            pl.BlockSpec((tf, d), lambda i, f: (f, 0)),
        ],
        out_specs=pl.BlockSpec((tm, d), lambda i, f: (i, 0)),
        out_shape=jax.ShapeDtypeStruct((t, d), F32),
        scratch_shapes=[pltpu.VMEM((tm, d), BF16)],
        compiler_params=pltpu.CompilerParams(
            dimension_semantics=("parallel", "arbitrary"), vmem_limit_bytes=VMEM_LIMIT),
        name="ffn",
    )(x, g2, w_up, w_down)


def _rope_tables(seq, rot_dim):
    half = rot_dim // 2
    inv = ROPE_THETA ** (-jnp.arange(0, rot_dim, 2, dtype=F32) / rot_dim)
    ang = jnp.arange(seq, dtype=F32)[:, None] * inv[None, :]
    cos, sin = jnp.cos(ang), jnp.sin(ang)
    pad = ATT_HD - rot_dim
    ones = jnp.ones((seq, pad), F32)
    zeros = jnp.zeros((seq, pad), F32)
    zh = jnp.zeros((seq, half), F32)
    rep = LANES // ATT_HD
    cos_t = jnp.tile(jnp.concatenate([cos, cos, ones], axis=1), (1, rep))
    sa_t = jnp.tile(jnp.concatenate([-sin, zh, zeros], axis=1), (1, rep))
    sb_t = jnp.tile(jnp.concatenate([zh, sin, zeros], axis=1), (1, rep))
    return cos_t, sa_t, sb_t


def _prep_layer(l, norm1_g, w_in, conv_w, q_norm_g, k_norm_g, lam_q1, lam_k1, lam_q2, lam_k2,
                subln_g, sgu_norm_g, sgu_w, sgu_b, w_out, norm2_g, w_up, w_down):
    d = w_in.shape[1]
    mlp_heads = sgu_w.shape[1]
    grp = jnp.arange(COL) // ATT_HD
    seg = jnp.where(grp[:, None] == grp[None, :], 1.0 / ATT_HD, 0.0).astype(BF16)
    return dict(
        g1=norm1_g[l].reshape(1, d),
        w_in=w_in[l].astype(BF16),
        conv_w=conv_w[l],
        gq=jnp.tile(q_norm_g[l], COL // ATT_HD).reshape(1, COL),
        gk=jnp.tile(k_norm_g[l], COL // ATT_HD).reshape(1, COL),
        seg=seg,
        lamv=jnp.stack([lam_q1[l], lam_k1[l], lam_q2[l], lam_k2[l]]).astype(F32),
        g_sub=subln_g[l].reshape(1, ATT_VD),
        gv=jnp.tile(sgu_norm_g[l], mlp_heads).reshape(1, mlp_heads * MLP_HD),
        ws=sgu_w[l].astype(BF16),
        bs=jnp.repeat(sgu_b[l].T, MLP_HD, axis=1),
        w_out=w_out[l].astype(BF16),
        g2=norm2_g[l].reshape(1, d),
        w_up=w_up[l].astype(BF16),
        w_down=w_down[l].astype(BF16),
        lambda_init=0.8 - 0.6 * math.exp(-0.3 * l),
    )


def _trunk(x, layers, tm=512, tq=256, tk=512, tn=512, tf=512):
    b, s, d = x.shape
    t = b * s
    heads = (d // 2) // ATT_VD
    tabs = _rope_tables(s, ATT_HD // 4)
    xf = x.reshape(t, d)
    for p in layers:
        proj = _in_proj(xf, p["g1"], p["w_in"], tabs, p["gq"], p["gk"], p["seg"], p["gv"],
                        p["ws"], p["bs"], seq=s, tm=tm)
        att = _attention(proj, p["lamv"], p["g_sub"], batch=b, seq=s, heads=heads,
                         q_col=12, k_col=12 + heads, v_col=12 + 2 * heads,
                         lambda_init=p["lambda_init"], tq=tq, tk=tk)
        x1 = _out_proj(proj, att, xf, p["conv_w"], p["w_out"], seq=s, tm=tm, tn=tn, oc_col=9)
        xf = _ffn(x1, p["g2"], p["w_up"], p["w_down"], tm=tm, tf=tf)
    return xf.reshape(b, s, d)


def kernel(x_prompt, x_sample, norm1_g, w_in, conv_w, q_norm_g, k_norm_g, lam_q1, lam_k1, lam_q2,
           lam_k2, subln_g, sgu_norm_g, sgu_w, sgu_b, w_out, norm2_g, w_up, w_down):
    params = (norm1_g, w_in, conv_w, q_norm_g, k_norm_g, lam_q1, lam_k1, lam_q2, lam_k2,
              subln_g, sgu_norm_g, sgu_w, sgu_b, w_out, norm2_g, w_up, w_down)
    layers = [_prep_layer(l, *params) for l in range(w_in.shape[0])]
    return (_trunk(x_prompt, layers), _trunk(x_sample, layers))
```

```python
import functools
import math

import jax
import jax.numpy as jnp
from jax import lax
from jax.experimental import pallas as pl
from jax.experimental.pallas import tpu as pltpu

F32 = jnp.float32
BF16 = jnp.bfloat16

EPS = 1e-6
ROPE_THETA = 500000.0
CONV_WIDTH = 3
ATT_HD = 64
ATT_VD = 2 * ATT_HD
CHUNK = 128
MLP_HD = 128
LANES = 128
BF16_SUBLANES = 16
COL = 512
VMEM_LIMIT = 56 * 1024 * 1024
LOG2E = math.log2(math.e)


def _rms(x, gain):
    ms = jnp.mean(x * x, axis=-1, keepdims=True)
    return x * lax.rsqrt(ms + EPS) * gain


def _in_proj_kernel(x_ref, g1_ref, w_ref, cos_ref, sa_ref, sb_ref, gq_ref, gk_ref, seg_ref,
                    gv_ref, ws_ref, bs_ref, o_ref, h_sc, u_sc, *, n_conv, n_qk, n_v):
    j = pl.program_id(1)
    tm = x_ref.shape[0]

    @pl.when(j == 0)
    def _():
        h_sc[...] = _rms(x_ref[...], g1_ref[...]).astype(BF16)

    acc = jnp.dot(h_sc[...], w_ref[...], preferred_element_type=F32)

    q_lo, k_lo = n_conv, n_conv + n_qk
    v_lo, u_at = k_lo + n_qk, k_lo + n_qk + n_v

    def qk_norm_rope(gain):
        ms = jnp.dot((acc * acc).astype(BF16), seg_ref[...], preferred_element_type=F32)
        y = acc * lax.rsqrt(ms + EPS) * gain
        cos, sa, sb = cos_ref[...], sa_ref[...], sb_ref[...]
        for c in range(COL // LANES):
            yc = y[:, c * LANES:(c + 1) * LANES]
            r = yc * cos + pltpu.roll(yc, LANES - 8, 1) * sa + pltpu.roll(yc, 8, 1) * sb
            o_ref[:, c * LANES:(c + 1) * LANES] = r.astype(BF16)

    @pl.when((j < q_lo) | ((j >= v_lo) & (j < u_at)))
    def _():
        o_ref[...] = acc.astype(BF16)

    @pl.when((j >= q_lo) & (j < k_lo))
    def _():
        qk_norm_rope(gq_ref[...] * (ATT_HD ** -0.5 * LOG2E))

    @pl.when((j >= k_lo) & (j < v_lo))
    def _():
        qk_norm_rope(gk_ref[...])

    @pl.when(j == u_at)
    def _():
        u_sc[...] = jax.nn.gelu(acc)

    @pl.when(j == u_at + 1)
    def _():
        v = jax.nn.gelu(acc)
        for h in range(COL // MLP_HD):
            sl = slice(h * MLP_HD, (h + 1) * MLP_HD)
            vn = _rms(v[:, sl], gv_ref[:, sl]).astype(BF16)
            w = ws_ref[h]
            for n in range(tm // CHUNK):
                rows = slice(n * CHUNK, (n + 1) * CHUNK)
                mixed = jnp.dot(w, vn[rows, :], preferred_element_type=F32) + bs_ref[:, sl]
                o_ref[rows, sl] = (u_sc[rows, sl] * mixed).astype(BF16)


def _in_proj(x, g1, w_in, tabs, gq, gk, seg, gv, ws, bs, *, seq, tm):
    t, d = x.shape
    n_in = w_in.shape[1] // COL
    n_out = n_in - 1
    pos_blocks = seq // tm
    const = lambda i, j: (0, 0)
    kern = functools.partial(_in_proj_kernel, n_conv=3, n_qk=2, n_v=2)
    return pl.pallas_call(
        kern,
        grid=(t // tm, n_in),
        in_specs=[
            pl.BlockSpec((tm, d), lambda i, j: (i, 0)),
            pl.BlockSpec((1, d), const),
            pl.BlockSpec((d, COL), lambda i, j: (0, j)),
            pl.BlockSpec((tm, LANES), lambda i, j: (i % pos_blocks, 0)),
            pl.BlockSpec((tm, LANES), lambda i, j: (i % pos_blocks, 0)),
            pl.BlockSpec((tm, LANES), lambda i, j: (i % pos_blocks, 0)),
            pl.BlockSpec((1, COL), const),
            pl.BlockSpec((1, COL), const),
            pl.BlockSpec((COL, COL), const),
            pl.BlockSpec((1, COL), const),
            pl.BlockSpec(ws.shape, lambda i, j: (0, 0, 0)),
            pl.BlockSpec((CHUNK, COL), const),
        ],
        out_specs=pl.BlockSpec((tm, COL), lambda i, j: (i, jnp.minimum(j, n_out - 1))),
        out_shape=jax.ShapeDtypeStruct((t, n_out * COL), BF16),
        scratch_shapes=[pltpu.VMEM((tm, d), BF16), pltpu.VMEM((tm, COL), F32)],
        compiler_params=pltpu.CompilerParams(
            dimension_semantics=("parallel", "arbitrary"), vmem_limit_bytes=VMEM_LIMIT),
        name="in_proj",
    )(x, g1, w_in, *tabs, gq, gk, seg, gv, ws, bs)


def _attn_kernel(lamv_ref, q_ref, k_ref, v_ref, g_ref, o_ref,
                 qst_sc, vt_sc, s_sc, p_sc, al_sc, m_sc, l_sc, acc_sc, *, lambda_init):
    i = pl.program_id(2)
    tq = q_ref.shape[0]
    nk, _, tk = vt_sc.shape
    r = 2 * tq

    @pl.when(i == 0)
    def _():
        for c in range(nk):
            vt_sc[c] = v_ref[c * tk:(c + 1) * tk, :].astype(F32).T.astype(BF16)

    qt = q_ref[...].astype(F32).T
    row = lax.broadcasted_iota(jnp.int32, qt.shape, 0)
    zero = jnp.zeros_like(qt)
    qst_sc[:, 0:tq] = jnp.where(row < ATT_HD, qt, zero).astype(BF16)
    qst_sc[:, tq:r] = jnp.where(row >= ATT_HD, qt, zero).astype(BF16)
    m_sc[...] = jnp.full(m_sc.shape, -jnp.inf, F32)
    l_sc[...] = jnp.zeros(l_sc.shape, F32)
    acc_sc[...] = jnp.zeros(acc_sc.shape, F32)

    def scores(j, slot):
        off = pl.multiple_of(j * tk, tk)
        s_sc[slot] = jnp.dot(k_ref[pl.ds(off, tk), :], qst_sc[...],
                             preferred_element_type=F32)

    def softmax(slot):
        s = s_sc[slot]
        m_prev = m_sc[...]
        m_new = jnp.maximum(m_prev, jnp.max(s, axis=0, keepdims=True))
        alpha = jnp.exp2(m_prev - m_new)
        p = jnp.exp2(s - m_new)
        l_sc[...] = alpha * l_sc[...] + jnp.sum(p, axis=0, keepdims=True)
        m_sc[...] = m_new
        al_sc[slot] = alpha
        p_sc[slot] = p.astype(BF16)

    def accumulate(j, slot):
        acc_sc[...] = al_sc[slot] * acc_sc[...] + jnp.dot(
            vt_sc[j], p_sc[slot], preferred_element_type=F32)

    scores(0, 0)
    softmax(0)
    scores(1, 1)

    def pair(jj, carry):
        j = 2 * jj + 1
        scores(j + 1, 0)
        softmax(1)
        accumulate(j - 1, 0)
        scores(j + 2, 1)
        softmax(0)
        accumulate(j, 1)
        return carry

    lax.fori_loop(0, (nk - 2) // 2, pair, 0)
    softmax(1)
    accumulate(nk - 2, 0)
    accumulate(nk - 1, 1)

    lv = lamv_ref[...]
    lam = (jnp.exp(jnp.sum(lv[0:1] * lv[1:2], axis=1, keepdims=True))
           - jnp.exp(jnp.sum(lv[2:3] * lv[3:4], axis=1, keepdims=True)) + lambda_init)
    o_all = acc_sc[...] / l_sc[...]
    ot = o_all[:, 0:tq] - lam * o_all[:, tq:r]
    ms = jnp.mean(ot * ot, axis=0, keepdims=True)
    o = (ot * lax.rsqrt(ms + EPS)).T * g_ref[...]
    o_ref[...] = (o * (1.0 - lambda_init)).astype(BF16)


def _attention(proj, lamv, g_sub, *, batch, seq, heads, q_col, k_col, v_col, lambda_init, tq, tk):
    nq = seq // tq
    nk = seq // tk
    assert nk >= 2 and nk % 2 == 0, "the key loop is unrolled in pairs"
    kern = functools.partial(_attn_kernel, lambda_init=lambda_init)
    return pl.pallas_call(
        kern,
        grid=(batch, heads, nq),
        in_specs=[
            pl.BlockSpec(lamv.shape, lambda b, h, i: (0, 0)),
            pl.BlockSpec((tq, ATT_VD), lambda b, h, i: (b * nq + i, q_col + h)),
            pl.BlockSpec((seq, ATT_VD), lambda b, h, i: (b, k_col + h)),
            pl.BlockSpec((seq, ATT_VD), lambda b, h, i: (b, v_col + h)),
            pl.BlockSpec((1, ATT_VD), lambda b, h, i: (0, 0)),
        ],
        out_specs=pl.BlockSpec((tq, ATT_VD), lambda b, h, i: (b * nq + i, h)),
        out_shape=jax.ShapeDtypeStruct((batch * seq, heads * ATT_VD), BF16),
        scratch_shapes=[
            pltpu.VMEM((ATT_VD, 2 * tq), BF16),
            pltpu.VMEM((nk, ATT_VD, tk), BF16),
            pltpu.VMEM((2, tk, 2 * tq), F32),
            pltpu.VMEM((2, tk, 2 * tq), BF16),
            pltpu.VMEM((2, 1, 2 * tq), F32),
            pltpu.VMEM((1, 2 * tq), F32),
            pltpu.VMEM((1, 2 * tq), F32),
            pltpu.VMEM((ATT_VD, 2 * tq), F32),
        ],
        compiler_params=pltpu.CompilerParams(
            dimension_semantics=("arbitrary", "arbitrary", "arbitrary"),
            vmem_limit_bytes=VMEM_LIMIT),
        name="diff_attention",
    )(lamv, proj, proj, proj, g_sub)


def _out_proj_kernel(ax_ref, ab_ref, ac_ref, axp_ref, acp_ref, axn_ref, acn_ref, cw_ref,
                     att_ref, oc_ref, x_ref, w_ref, o_ref, mix_sc, *, pos_blocks):
    i = pl.program_id(0)
    j = pl.program_id(1)
    tm, cc = ax_ref.shape

    @pl.when(j == 0)
    def _():
        z = ac_ref[...].astype(F32) * ax_ref[...].astype(F32)
        last = BF16_SUBLANES - 1
        zp = acp_ref[last:last + 1, :].astype(F32) * axp_ref[last:last + 1, :].astype(F32)
        zn = acn_ref[0:1, :].astype(F32) * axn_ref[0:1, :].astype(F32)
        zp = jnp.where(i % pos_blocks == 0, jnp.zeros_like(zp), zp)
        zn = jnp.where(i % pos_blocks == pos_blocks - 1, jnp.zeros_like(zn), zn)
        row = lax.broadcasted_iota(jnp.int32, z.shape, 0)
        z_prev = jnp.where(row == 0, zp, pltpu.roll(z, 1, 0))
        z_next = jnp.where(row == tm - 1, zn, pltpu.roll(z, tm - 1, 0))
        cw = cw_ref[...]
        conv = cw[0:1] * z_prev + cw[1:2] * z + cw[2:3] * z_next
        mix_sc[:, 0:cc] = (ab_ref[...].astype(F32) * conv).astype(BF16)
        aw = att_ref.shape[1]
        mix_sc[:, cc:cc + aw] = att_ref[...]
        mix_sc[:, cc + aw:] = oc_ref[...]

    o_ref[...] = x_ref[...] + jnp.dot(mix_sc[...], w_ref[...], preferred_element_type=F32)


def _out_proj(proj, att, x, conv_w, w_out, *, seq, tm, tn, oc_col):
    t, d = x.shape
    pos_blocks = seq // tm
    hb = tm // BF16_SUBLANES
    n_hb = t // BF16_SUBLANES
    prev = lambda i: jnp.maximum(i * hb - 1, 0)
    nxt = lambda i: jnp.minimum((i + 1) * hb, n_hb - 1)
    kern = functools.partial(_out_proj_kernel, pos_blocks=pos_blocks)
    return pl.pallas_call(
        kern,
        grid=(t // tm, d // tn),
        in_specs=[
            pl.BlockSpec((tm, COL), lambda i, j: (i, 0)),
            pl.BlockSpec((tm, COL), lambda i, j: (i, 1)),
            pl.BlockSpec((tm, COL), lambda i, j: (i, 2)),
            pl.BlockSpec((BF16_SUBLANES, COL), lambda i, j: (prev(i), 0)),
            pl.BlockSpec((BF16_SUBLANES, COL), lambda i, j: (prev(i), 2)),
            pl.BlockSpec((BF16_SUBLANES, COL), lambda i, j: (nxt(i), 0)),
            pl.BlockSpec((BF16_SUBLANES, COL), lambda i, j: (nxt(i), 2)),
            pl.BlockSpec(conv_w.shape, lambda i, j: (0, 0)),
            pl.BlockSpec((tm, att.shape[1]), lambda i, j: (i, 0)),
            pl.BlockSpec((tm, COL), lambda i, j: (i, oc_col)),
            pl.BlockSpec((tm, tn), lambda i, j: (i, j)),
            pl.BlockSpec((w_out.shape[0], tn), lambda i, j: (0, j)),
        ],
        out_specs=pl.BlockSpec((tm, tn), lambda i, j: (i, j)),
        out_shape=jax.ShapeDtypeStruct((t, d), F32),
        scratch_shapes=[pltpu.VMEM((tm, w_out.shape[0]), BF16)],
        compiler_params=pltpu.CompilerParams(
            dimension_semantics=("parallel", "arbitrary"), vmem_limit_bytes=VMEM_LIMIT),
        name="out_proj",
    )(proj, proj, proj, proj, proj, proj, proj, conv_w, att, proj, x, w_out)


def _ffn_kernel(x_ref, g_ref, wu_ref, wd_ref, o_ref, h_sc):
    f = pl.program_id(1)

    @pl.when(f == 0)
    def _():
        x = x_ref[...]
        h_sc[...] = _rms(x, g_ref[...]).astype(BF16)
        o_ref[...] = x

    hid = jnp.dot(h_sc[...], wu_ref[...], preferred_element_type=F32)
    hid = jnp.square(jnp.maximum(hid, 0.0)).astype(BF16)
    o_ref[...] += jnp.dot(hid, wd_ref[...], preferred_element_type=F32)


def _ffn(x, g2, w_up, w_down, *, tm, tf):
    t, d = x.shape
    ff = w_up.shape[1]
    return pl.pallas_call(
        _ffn_kernel,
        grid=(t // tm, ff // tf),
        in_specs=[
            pl.BlockSpec((tm, d), lambda i, f: (i, 0)),
            pl.BlockSpec((1, d), lambda i, f: (0, 0)),
            pl.BlockSpec((d, tf), lambda i, f: (0, f)),
            pl.BlockSpec((tf, d), lambda i, f: (f, 0)),
        ],
        out_specs=pl.BlockSpec((tm, d), lambda i, f: (i, 0)),
        out_shape=jax.ShapeDtypeStruct((t, d), F32),
        scratch_shapes=[pltpu.VMEM((tm, d), BF16)],
        compiler_params=pltpu.CompilerParams(
            dimension_semantics=("parallel", "arbitrary"), vmem_limit_bytes=VMEM_LIMIT),
        name="ffn",
    )(x, g2, w_up, w_down)


def _rope_tables(seq, rot_dim):
    half = rot_dim // 2
    inv = ROPE_THETA ** (-jnp.arange(0, rot_dim, 2, dtype=F32) / rot_dim)
    ang = jnp.arange(seq, dtype=F32)[:, None] * inv[None, :]
    cos, sin = jnp.cos(ang), jnp.sin(ang)
    pad = ATT_HD - rot_dim
    ones = jnp.ones((seq, pad), F32)
    zeros = jnp.zeros((seq, pad), F32)
    zh = jnp.zeros((seq, half), F32)
    rep = LANES // ATT_HD
    cos_t = jnp.tile(jnp.concatenate([cos, cos, ones], axis=1), (1, rep))
    sa_t = jnp.tile(jnp.concatenate([-sin, zh, zeros], axis=1), (1, rep))
    sb_t = jnp.tile(jnp.concatenate([zh, sin, zeros], axis=1), (1, rep))
    return cos_t, sa_t, sb_t


def _prep_layer(l, norm1_g, w_in, conv_w, q_norm_g, k_norm_g, lam_q1, lam_k1, lam_q2, lam_k2,
                subln_g, sgu_norm_g, sgu_w, sgu_b, w_out, norm2_g, w_up, w_down):
    d = w_in.shape[1]
    mlp_heads = sgu_w.shape[1]
    grp = jnp.arange(COL) // ATT_HD
    seg = jnp.where(grp[:, None] == grp[None, :], 1.0 / ATT_HD, 0.0).astype(BF16)
    return dict(
        g1=norm1_g[l].reshape(1, d),
        w_in=w_in[l].astype(BF16),
        conv_w=conv_w[l],
        gq=jnp.tile(q_norm_g[l], COL // ATT_HD).reshape(1, COL),
        gk=jnp.tile(k_norm_g[l], COL // ATT_HD).reshape(1, COL),
        seg=seg,
        lamv=jnp.stack([lam_q1[l], lam_k1[l], lam_q2[l], lam_k2[l]]).astype(F32),
        g_sub=subln_g[l].reshape(1, ATT_VD),
        gv=jnp.tile(sgu_norm_g[l], mlp_heads).reshape(1, mlp_heads * MLP_HD),
        ws=sgu_w[l].astype(BF16),
        bs=jnp.repeat(sgu_b[l].T, MLP_HD, axis=1),
        w_out=w_out[l].astype(BF16),
        g2=norm2_g[l].reshape(1, d),
        w_up=w_up[l].astype(BF16),
        w_down=w_down[l].astype(BF16),
        lambda_init=0.8 - 0.6 * math.exp(-0.3 * l),
    )


def _trunk(x, layers, tm=512, tq=256, tk=512, tn=512, tf=512):
    b, s, d = x.shape
    t = b * s
    heads = (d // 2) // ATT_VD
    tabs = _rope_tables(s, ATT_HD // 4)
    xf = x.reshape(t, d)
    for p in layers:
        proj = _in_proj(xf, p["g1"], p["w_in"], tabs, p["gq"], p["gk"], p["seg"], p["gv"],
                        p["ws"], p["bs"], seq=s, tm=tm)
        att = _attention(proj, p["lamv"], p["g_sub"], batch=b, seq=s, heads=heads,
                         q_col=12, k_col=12 + heads, v_col=12 + 2 * heads,
                         lambda_init=p["lambda_init"], tq=tq, tk=tk)
        x1 = _out_proj(proj, att, xf, p["conv_w"], p["w_out"], seq=s, tm=tm, tn=tn, oc_col=9)
        xf = _ffn(x1, p["g2"], p["w_up"], p["w_down"], tm=tm, tf=tf)
    return xf.reshape(b, s, d)


def kernel(x_prompt, x_sample, norm1_g, w_in, conv_w, q_norm_g, k_norm_g, lam_q1, lam_k1, lam_q2,
           lam_k2, subln_g, sgu_norm_g, sgu_w, sgu_b, w_out, norm2_g, w_up, w_down):
    params = (norm1_g, w_in, conv_w, q_norm_g, k_norm_g, lam_q1, lam_k1, lam_q2, lam_k2,
              subln_g, sgu_norm_g, sgu_w, sgu_b, w_out, norm2_g, w_up, w_down)
    layers = [_prep_layer(l, *params) for l in range(w_in.shape[0])]
    return (_trunk(x_prompt, layers), _trunk(x_sample, layers))
```

```python
import functools
import math

import jax
import jax.numpy as jnp
from jax import lax
from jax.experimental import pallas as pl
from jax.experimental.pallas import tpu as pltpu

F32 = jnp.float32
BF16 = jnp.bfloat16

EPS = 1e-6
ROPE_THETA = 500000.0
CONV_WIDTH = 3
ATT_HD = 64
ATT_VD = 2 * ATT_HD
KEY_CHUNK = 128
TRIPS_PER_LOOP = 2
CHUNK = 128
MLP_HD = 128
LANES = 128
BF16_SUBLANES = 16
COL = 512
VMEM_LIMIT = 56 * 1024 * 1024
LOG2E = math.log2(math.e)


def _rms(x, gain):
    ms = jnp.mean(x * x, axis=-1, keepdims=True)
    return x * lax.rsqrt(ms + EPS) * gain


def _in_proj_kernel(x_ref, g1_ref, w_ref, cos_ref, sa_ref, sb_ref, gq_ref, gk_ref, seg_ref,
                    gv_ref, ws_ref, bs_ref, o_ref, h_sc, u_sc, *, n_conv, n_qk, n_v):
    j = pl.program_id(1)
    tm = x_ref.shape[0]

    @pl.when(j == 0)
    def _():
        h_sc[...] = _rms(x_ref[...], g1_ref[...]).astype(BF16)

    acc = jnp.dot(h_sc[...], w_ref[...], preferred_element_type=F32)

    q_lo, k_lo = n_conv, n_conv + n_qk
    v_lo, u_at = k_lo + n_qk, k_lo + n_qk + n_v

    def qk_norm_rope(gain):
        ms = jnp.dot((acc * acc).astype(BF16), seg_ref[...], preferred_element_type=F32)
        y = acc * lax.rsqrt(ms + EPS) * gain
        cos, sa, sb = cos_ref[...], sa_ref[...], sb_ref[...]
        for c in range(COL // LANES):
            yc = y[:, c * LANES:(c + 1) * LANES]
            r = yc * cos + pltpu.roll(yc, LANES - 8, 1) * sa + pltpu.roll(yc, 8, 1) * sb
            o_ref[:, c * LANES:(c + 1) * LANES] = r.astype(BF16)

    @pl.when((j < q_lo) | ((j >= v_lo) & (j < u_at)))
    def _():
        o_ref[...] = acc.astype(BF16)

    @pl.when((j >= q_lo) & (j < k_lo))
    def _():
        qk_norm_rope(gq_ref[...] * (ATT_HD ** -0.5 * LOG2E))

    @pl.when((j >= k_lo) & (j < v_lo))
    def _():
        qk_norm_rope(gk_ref[...])

    @pl.when(j == u_at)
    def _():
        u_sc[...] = jax.nn.gelu(acc)

    @pl.when(j == u_at + 1)
    def _():
        v = jax.nn.gelu(acc)
        for h in range(COL // MLP_HD):
            sl = slice(h * MLP_HD, (h + 1) * MLP_HD)
            vn = _rms(v[:, sl], gv_ref[:, sl]).astype(BF16)
            w = ws_ref[h]
            for n in range(tm // CHUNK):
                rows = slice(n * CHUNK, (n + 1) * CHUNK)
                mixed = jnp.dot(w, vn[rows, :], preferred_element_type=F32) + bs_ref[:, sl]
                o_ref[rows, sl] = (u_sc[rows, sl] * mixed).astype(BF16)


def _in_proj(x, g1, w_in, tabs, gq, gk, seg, gv, ws, bs, *, seq, tm):
    t, d = x.shape
    n_in = w_in.shape[1] // COL
    n_out = n_in - 1
    pos_blocks = seq // tm
    const = lambda i, j: (0, 0)
    kern = functools.partial(_in_proj_kernel, n_conv=3, n_qk=2, n_v=2)
    return pl.pallas_call(
        kern,
        grid=(t // tm, n_in),
        in_specs=[
            pl.BlockSpec((tm, d), lambda i, j: (i, 0)),
            pl.BlockSpec((1, d), const),
            pl.BlockSpec((d, COL), lambda i, j: (0, j)),
            pl.BlockSpec((tm, LANES), lambda i, j: (i % pos_blocks, 0)),
            pl.BlockSpec((tm, LANES), lambda i, j: (i % pos_blocks, 0)),
            pl.BlockSpec((tm, LANES), lambda i, j: (i % pos_blocks, 0)),
            pl.BlockSpec((1, COL), const),
            pl.BlockSpec((1, COL), const),
            pl.BlockSpec((COL, COL), const),
            pl.BlockSpec((1, COL), const),
            pl.BlockSpec(ws.shape, lambda i, j: (0, 0, 0)),
            pl.BlockSpec((CHUNK, COL), const),
        ],
        out_specs=pl.BlockSpec((tm, COL), lambda i, j: (i, jnp.minimum(j, n_out - 1))),
        out_shape=jax.ShapeDtypeStruct((t, n_out * COL), BF16),
        scratch_shapes=[pltpu.VMEM((tm, d), BF16), pltpu.VMEM((tm, COL), F32)],
        compiler_params=pltpu.CompilerParams(
            dimension_semantics=("parallel", "arbitrary"), vmem_limit_bytes=VMEM_LIMIT),
        name="in_proj",
    )(x, g1, w_in, *tabs, gq, gk, seg, gv, ws, bs)


def _attn_kernel(lamv_ref, q_ref, k_ref, v_ref, g_ref, o_ref,
                 qst_sc, vt_sc, s_sc, mx_sc, p_sc, al_sc, m_sc, acc_sc, *, lambda_init):
    i = pl.program_id(2)
    tq = q_ref.shape[0]
    nk, _, tk = vt_sc.shape
    r = 2 * tq

    @pl.when(i == 0)
    def _():
        ones = jnp.ones((BF16_SUBLANES, tk), BF16)
        for c in range(nk):
            vt_sc[c, 0:ATT_VD, :] = v_ref[c * tk:(c + 1) * tk, :].astype(F32).T.astype(BF16)
            vt_sc[c, ATT_VD:, :] = ones

    qt = q_ref[...].astype(F32).T
    row = lax.broadcasted_iota(jnp.int32, qt.shape, 0)
    zero = jnp.zeros_like(qt)
    qst_sc[:, 0:tq] = jnp.where(row < ATT_HD, qt, zero).astype(BF16)
    qst_sc[:, tq:r] = jnp.where(row >= ATT_HD, qt, zero).astype(BF16)
    m_sc[...] = jnp.full(m_sc.shape, -jnp.inf, F32)
    acc_sc[...] = jnp.zeros(acc_sc.shape, F32)

    n_chunks = tk // KEY_CHUNK

    def trip(score_blk=None, softmax_slot=None, acc_blk=None):
        if softmax_slot is not None:
            m_prev = m_sc[...]
            m_new = jnp.maximum(m_prev, mx_sc[softmax_slot])
            m_sc[...] = m_new
            al_sc[softmax_slot] = jnp.exp2(m_prev - m_new)
        mx = None
        for c in range(n_chunks):
            rows = slice(c * KEY_CHUNK, (c + 1) * KEY_CHUNK)
            if score_blk is not None:
                j, slot = score_blk
                off = pl.multiple_of(j * tk + c * KEY_CHUNK, KEY_CHUNK)
                s = jnp.dot(k_ref[pl.ds(off, KEY_CHUNK), :], qst_sc[...],
                            preferred_element_type=F32)
                s_sc[slot, rows, :] = s
                cmx = jnp.max(s, axis=0, keepdims=True)
                mx = cmx if mx is None else jnp.maximum(mx, cmx)
            if softmax_slot is not None:
                p_sc[softmax_slot, rows, :] = jnp.exp2(
                    s_sc[softmax_slot, rows, :] - m_new).astype(BF16)
            if acc_blk is not None and c == max(n_chunks // 2 - 1, 0):
                j, slot = acc_blk
                acc_sc[...] = al_sc[slot] * acc_sc[...] + jnp.dot(
                    vt_sc[j], p_sc[slot], preferred_element_type=F32)
        if score_blk is not None:
            mx_sc[score_blk[1]] = mx

    def full_trip(j, parity):
        trip(score_blk=(j + 1, 1 - parity), softmax_slot=parity, acc_blk=(j - 1, 1 - parity))

    trip(score_blk=(0, 0))
    trip(score_blk=(1, 1), softmax_slot=0)

    n_full = nk - 2
    n_loops = n_full // TRIPS_PER_LOOP

    def unrolled(jj, carry):
        j0 = 1 + TRIPS_PER_LOOP * jj
        for u in range(TRIPS_PER_LOOP):
            full_trip(j0 + u, (1 + u) % 2)
        return carry

    lax.fori_loop(0, n_loops, unrolled, 0)
    for j in range(1 + n_loops * TRIPS_PER_LOOP, nk - 1):
        full_trip(j, j % 2)
    trip(softmax_slot=(nk - 1) % 2, acc_blk=(nk - 2, nk % 2))
    trip(acc_blk=(nk - 1, (nk - 1) % 2))

    lv = lamv_ref[...]
    lam = (jnp.exp(jnp.sum(lv[0:1] * lv[1:2], axis=1, keepdims=True))
           - jnp.exp(jnp.sum(lv[2:3] * lv[3:4], axis=1, keepdims=True)) + lambda_init)
    o_all = acc_sc[0:ATT_VD, :] / acc_sc[ATT_VD:ATT_VD + 1, :]
    ot = o_all[:, 0:tq] - lam * o_all[:, tq:r]
    ms = jnp.mean(ot * ot, axis=0, keepdims=True)
    o = (ot * lax.rsqrt(ms + EPS)).T * g_ref[...]
    o_ref[...] = (o * (1.0 - lambda_init)).astype(BF16)


def _attention(proj, lamv, g_sub, *, batch, seq, heads, q_col, k_col, v_col, lambda_init, tq, tk):
    nq = seq // tq
    nk = seq // tk
    assert nk >= 2 and TRIPS_PER_LOOP % 2 == 0, "slots alternate with block parity"
    kern = functools.partial(_attn_kernel, lambda_init=lambda_init)
    return pl.pallas_call(
        kern,
        grid=(batch, heads, nq),
        in_specs=[
            pl.BlockSpec(lamv.shape, lambda b, h, i: (0, 0)),
            pl.BlockSpec((tq, ATT_VD), lambda b, h, i: (b * nq + i, q_col + h)),
            pl.BlockSpec((seq, ATT_VD), lambda b, h, i: (b, k_col + h)),
            pl.BlockSpec((seq, ATT_VD), lambda b, h, i: (b, v_col + h)),
            pl.BlockSpec((1, ATT_VD), lambda b, h, i: (0, 0)),
        ],
        out_specs=pl.BlockSpec((tq, ATT_VD), lambda b, h, i: (b * nq + i, h)),
        out_shape=jax.ShapeDtypeStruct((batch * seq, heads * ATT_VD), BF16),
        scratch_shapes=[
            pltpu.VMEM((ATT_VD, 2 * tq), BF16),
            pltpu.VMEM((nk, ATT_VD + BF16_SUBLANES, tk), BF16),
            pltpu.VMEM((2, tk, 2 * tq), F32),
            pltpu.VMEM((2, 1, 2 * tq), F32),
            pltpu.VMEM((2, tk, 2 * tq), BF16),
            pltpu.VMEM((2, 1, 2 * tq), F32),
            pltpu.VMEM((1, 2 * tq), F32),
            pltpu.VMEM((ATT_VD + BF16_SUBLANES, 2 * tq), F32),
        ],
        compiler_params=pltpu.CompilerParams(
            dimension_semantics=("arbitrary", "arbitrary", "arbitrary"),
            vmem_limit_bytes=VMEM_LIMIT),
        name="diff_attention",
    )(lamv, proj, proj, proj, g_sub)


def _out_proj_kernel(ax_ref, ab_ref, ac_ref, axp_ref, acp_ref, axn_ref, acn_ref, cw_ref,
                     att_ref, oc_ref, x_ref, w_ref, o_ref, mix_sc, *, pos_blocks):
    i = pl.program_id(0)
    j = pl.program_id(1)
    tm, cc = ax_ref.shape

    @pl.when(j == 0)
    def _():
        z = ac_ref[...].astype(F32) * ax_ref[...].astype(F32)
        last = BF16_SUBLANES - 1
        zp = acp_ref[last:last + 1, :].astype(F32) * axp_ref[last:last + 1, :].astype(F32)
        zn = acn_ref[0:1, :].astype(F32) * axn_ref[0:1, :].astype(F32)
        zp = jnp.where(i % pos_blocks == 0, jnp.zeros_like(zp), zp)
        zn = jnp.where(i % pos_blocks == pos_blocks - 1, jnp.zeros_like(zn), zn)
        row = lax.broadcasted_iota(jnp.int32, z.shape, 0)
        z_prev = jnp.where(row == 0, zp, pltpu.roll(z, 1, 0))
        z_next = jnp.where(row == tm - 1, zn, pltpu.roll(z, tm - 1, 0))
        cw = cw_ref[...]
        conv = cw[0:1] * z_prev + cw[1:2] * z + cw[2:3] * z_next
        mix_sc[:, 0:cc] = (ab_ref[...].astype(F32) * conv).astype(BF16)
        aw = att_ref.shape[1]
        mix_sc[:, cc:cc + aw] = att_ref[...]
        mix_sc[:, cc + aw:] = oc_ref[...]

    o_ref[...] = x_ref[...] + jnp.dot(mix_sc[...], w_ref[...], preferred_element_type=F32)


def _out_proj(proj, att, x, conv_w, w_out, *, seq, tm, tn, oc_col):
    t, d = x.shape
    pos_blocks = seq // tm
    hb = tm // BF16_SUBLANES
    n_hb = t // BF16_SUBLANES
    prev = lambda i: jnp.maximum(i * hb - 1, 0)
    nxt = lambda i: jnp.minimum((i + 1) * hb, n_hb - 1)
    kern = functools.partial(_out_proj_kernel, pos_blocks=pos_blocks)
    return pl.pallas_call(
        kern,
        grid=(t // tm, d // tn),
        in_specs=[
            pl.BlockSpec((tm, COL), lambda i, j: (i, 0)),
            pl.BlockSpec((tm, COL), lambda i, j: (i, 1)),
            pl.BlockSpec((tm, COL), lambda i, j: (i, 2)),
            pl.BlockSpec((BF16_SUBLANES, COL), lambda i, j: (prev(i), 0)),
            pl.BlockSpec((BF16_SUBLANES, COL), lambda i, j: (prev(i), 2)),
            pl.BlockSpec((BF16_SUBLANES, COL), lambda i, j: (nxt(i), 0)),
            pl.BlockSpec((BF16_SUBLANES, COL), lambda i, j: (nxt(i), 2)),
            pl.BlockSpec(conv_w.shape, lambda i, j: (0, 0)),
            pl.BlockSpec((tm, att.shape[1]), lambda i, j: (i, 0)),
            pl.BlockSpec((tm, COL), lambda i, j: (i, oc_col)),
            pl.BlockSpec((tm, tn), lambda i, j: (i, j)),
            pl.BlockSpec((w_out.shape[0], tn), lambda i, j: (0, j)),
        ],
        out_specs=pl.BlockSpec((tm, tn), lambda i, j: (i, j)),
        out_shape=jax.ShapeDtypeStruct((t, d), F32),
        scratch_shapes=[pltpu.VMEM((tm, w_out.shape[0]), BF16)],
        compiler_params=pltpu.CompilerParams(
            dimension_semantics=("parallel", "arbitrary"), vmem_limit_bytes=VMEM_LIMIT),
        name="out_proj",
    )(proj, proj, proj, proj, proj, proj, proj, conv_w, att, proj, x, w_out)


def _ffn_kernel(x_ref, g_ref, wu_ref, wd_ref, o_ref, h_sc):
    f = pl.program_id(1)

    @pl.when(f == 0)
    def _():
        x = x_ref[...]
        h_sc[...] = _rms(x, g_ref[...]).astype(BF16)
        o_ref[...] = x

    hid = jnp.dot(h_sc[...], wu_ref[...], preferred_element_type=F32)
    hid = jnp.square(jnp.maximum(hid, 0.0)).astype(BF16)
    o_ref[...] += jnp.dot(hid, wd_ref[...], preferred_element_type=F32)


def _ffn(x, g2, w_up, w_down, *, tm, tf):
    t, d = x.shape
    ff = w_up.shape[1]
    return pl.pallas_call(
        _ffn_kernel,
        grid=(t // tm, ff // tf),
        in_specs=[
            pl.BlockSpec((tm, d), lambda i, f: (i, 0)),
            pl.BlockSpec((1, d), lambda i, f: (0, 0)),
            pl.BlockSpec((d, tf), lambda i, f: (0, f)),
            pl.BlockSpec((tf, d), lambda i, f: (f, 0)),
        ],
        out_specs=pl.BlockSpec((tm, d), lambda i, f: (i, 0)),
        out_shape=jax.ShapeDtypeStruct((t, d), F32),
        scratch_shapes=[pltpu.VMEM((tm, d), BF16)],
        compiler_params=pltpu.CompilerParams(
            dimension_semantics=("parallel", "arbitrary"), vmem_limit_bytes=VMEM_LIMIT),
        name="ffn",
    )(x, g2, w_up, w_down)


def _rope_tables(seq, rot_dim):
    half = rot_dim // 2
    inv = ROPE_THETA ** (-jnp.arange(0, rot_dim, 2, dtype=F32) / rot_dim)
    ang = jnp.arange(seq, dtype=F32)[:, None] * inv[None, :]
    cos, sin = jnp.cos(ang), jnp.sin(ang)
    pad = ATT_HD - rot_dim
    ones = jnp.ones((seq, pad), F32)
    zeros = jnp.zeros((seq, pad), F32)
    zh = jnp.zeros((seq, half), F32)
    rep = LANES // ATT_HD
    cos_t = jnp.tile(jnp.concatenate([cos, cos, ones], axis=1), (1, rep))
    sa_t = jnp.tile(jnp.concatenate([-sin, zh, zeros], axis=1), (1, rep))
    sb_t = jnp.tile(jnp.concatenate([zh, sin, zeros], axis=1), (1, rep))
    return cos_t, sa_t, sb_t


def _prep_layer(l, norm1_g, w_in, conv_w, q_norm_g, k_norm_g, lam_q1, lam_k1, lam_q2, lam_k2,
                subln_g, sgu_norm_g, sgu_w, sgu_b, w_out, norm2_g, w_up, w_down):
    d = w_in.shape[1]
    mlp_heads = sgu_w.shape[1]
    grp = jnp.arange(COL) // ATT_HD
    seg = jnp.where(grp[:, None] == grp[None, :], 1.0 / ATT_HD, 0.0).astype(BF16)
    return dict(
        g1=norm1_g[l].reshape(1, d),
        w_in=w_in[l].astype(BF16),
        conv_w=conv_w[l],
        gq=jnp.tile(q_norm_g[l], COL // ATT_HD).reshape(1, COL),
        gk=jnp.tile(k_norm_g[l], COL // ATT_HD).reshape(1, COL),
        seg=seg,
        lamv=jnp.stack([lam_q1[l], lam_k1[l], lam_q2[l], lam_k2[l]]).astype(F32),
        g_sub=subln_g[l].reshape(1, ATT_VD),
        gv=jnp.tile(sgu_norm_g[l], mlp_heads).reshape(1, mlp_heads * MLP_HD),
        ws=sgu_w[l].astype(BF16),
        bs=jnp.repeat(sgu_b[l].T, MLP_HD, axis=1),
        w_out=w_out[l].astype(BF16),
        g2=norm2_g[l].reshape(1, d),
        w_up=w_up[l].astype(BF16),
        w_down=w_down[l].astype(BF16),
        lambda_init=0.8 - 0.6 * math.exp(-0.3 * l),
    )


def _trunk(x, layers, tm=512, tq=256, tk=512, tn=512, tf=512):
    b, s, d = x.shape
    t = b * s
    heads = (d // 2) // ATT_VD
    tabs = _rope_tables(s, ATT_HD // 4)
    xf = x.reshape(t, d)
    for p in layers:
        proj = _in_proj(xf, p["g1"], p["w_in"], tabs, p["gq"], p["gk"], p["seg"], p["gv"],
                        p["ws"], p["bs"], seq=s, tm=tm)
        att = _attention(proj, p["lamv"], p["g_sub"], batch=b, seq=s, heads=heads,
                         q_col=12, k_col=12 + heads, v_col=12 + 2 * heads,
                         lambda_init=p["lambda_init"], tq=tq, tk=tk)
        x1 = _out_proj(proj, att, xf, p["conv_w"], p["w_out"], seq=s, tm=tm, tn=tn, oc_col=9)
        xf = _ffn(x1, p["g2"], p["w_up"], p["w_down"], tm=tm, tf=tf)
    return xf.reshape(b, s, d)


def kernel(x_prompt, x_sample, norm1_g, w_in, conv_w, q_norm_g, k_norm_g, lam_q1, lam_k1, lam_q2,
           lam_k2, subln_g, sgu_norm_g, sgu_w, sgu_b, w_out, norm2_g, w_up, w_down):
    params = (norm1_g, w_in, conv_w, q_norm_g, k_norm_g, lam_q1, lam_k1, lam_q2, lam_k2,
              subln_g, sgu_norm_g, sgu_w, sgu_b, w_out, norm2_g, w_up, w_down)
    layers = [_prep_layer(l, *params) for l in range(w_in.shape[0])]
    return (_trunk(x_prompt, layers), _trunk(x_sample, layers))
```

```python
import functools
import math

import jax
import jax.numpy as jnp
from jax import lax
from jax.experimental import pallas as pl
from jax.experimental.pallas import tpu as pltpu

F32 = jnp.float32
BF16 = jnp.bfloat16

EPS = 1e-6
ROPE_THETA = 500000.0
CONV_WIDTH = 3
ATT_HD = 64
ATT_VD = 2 * ATT_HD
KEY_CHUNK = 128
TRIPS_PER_LOOP = 4
TRIP_TRACE_ORDER = ("S0", "E0", "S1", "E1", "V0", "S2", "E2", "S3", "E3")
CHUNK = 128
MLP_HD = 128
LANES = 128
BF16_SUBLANES = 16
COL = 512
VMEM_LIMIT = 56 * 1024 * 1024
LOG2E = math.log2(math.e)


def _rms(x, gain):
    ms = jnp.mean(x * x, axis=-1, keepdims=True)
    return x * lax.rsqrt(ms + EPS) * gain


def _in_proj_kernel(x_ref, g1_ref, w_ref, cos_ref, sa_ref, sb_ref, gq_ref, gk_ref, seg_ref,
                    gv_ref, ws_ref, bs_ref, o_ref, h_sc, u_sc, *, n_conv, n_qk, n_v):
    j = pl.program_id(1)
    tm = x_ref.shape[0]

    @pl.when(j == 0)
    def _():
        h_sc[...] = _rms(x_ref[...], g1_ref[...]).astype(BF16)

    acc = jnp.dot(h_sc[...], w_ref[...], preferred_element_type=F32)

    q_lo, k_lo = n_conv, n_conv + n_qk
    v_lo, u_at = k_lo + n_qk, k_lo + n_qk + n_v

    def qk_norm_rope(gain):
        ms = jnp.dot((acc * acc).astype(BF16), seg_ref[...], preferred_element_type=F32)
        y = acc * lax.rsqrt(ms + EPS) * gain
        cos, sa, sb = cos_ref[...], sa_ref[...], sb_ref[...]
        for c in range(COL // LANES):
            yc = y[:, c * LANES:(c + 1) * LANES]
            r = yc * cos + pltpu.roll(yc, LANES - 8, 1) * sa + pltpu.roll(yc, 8, 1) * sb
            o_ref[:, c * LANES:(c + 1) * LANES] = r.astype(BF16)

    @pl.when((j < q_lo) | ((j >= v_lo) & (j < u_at)))
    def _():
        o_ref[...] = acc.astype(BF16)

    @pl.when((j >= q_lo) & (j < k_lo))
    def _():
        qk_norm_rope(gq_ref[...] * (ATT_HD ** -0.5 * LOG2E))

    @pl.when((j >= k_lo) & (j < v_lo))
    def _():
        qk_norm_rope(gk_ref[...])

    @pl.when(j == u_at)
    def _():
        u_sc[...] = jax.nn.gelu(acc)

    @pl.when(j == u_at + 1)
    def _():
        v = jax.nn.gelu(acc)
        for h in range(COL // MLP_HD):
            sl = slice(h * MLP_HD, (h + 1) * MLP_HD)
            vn = _rms(v[:, sl], gv_ref[:, sl]).astype(BF16)
            w = ws_ref[h]
            for n in range(tm // CHUNK):
                rows = slice(n * CHUNK, (n + 1) * CHUNK)
                mixed = jnp.dot(w, vn[rows, :], preferred_element_type=F32) + bs_ref[:, sl]
                o_ref[rows, sl] = (u_sc[rows, sl] * mixed).astype(BF16)


def _in_proj(x, g1, w_in, tabs, gq, gk, seg, gv, ws, bs, *, seq, tm):
    t, d = x.shape
    n_in = w_in.shape[1] // COL
    n_out = n_in - 1
    pos_blocks = seq // tm
    const = lambda i, j: (0, 0)
    kern = functools.partial(_in_proj_kernel, n_conv=3, n_qk=2, n_v=2)
    return pl.pallas_call(
        kern,
        grid=(t // tm, n_in),
        in_specs=[
            pl.BlockSpec((tm, d), lambda i, j: (i, 0)),
            pl.BlockSpec((1, d), const),
            pl.BlockSpec((d, COL), lambda i, j: (0, j)),
            pl.BlockSpec((tm, LANES), lambda i, j: (i % pos_blocks, 0)),
            pl.BlockSpec((tm, LANES), lambda i, j: (i % pos_blocks, 0)),
            pl.BlockSpec((tm, LANES), lambda i, j: (i % pos_blocks, 0)),
            pl.BlockSpec((1, COL), const),
            pl.BlockSpec((1, COL), const),
            pl.BlockSpec((COL, COL), const),
            pl.BlockSpec((1, COL), const),
            pl.BlockSpec(ws.shape, lambda i, j: (0, 0, 0)),
            pl.BlockSpec((CHUNK, COL), const),
        ],
        out_specs=pl.BlockSpec((tm, COL), lambda i, j: (i, jnp.minimum(j, n_out - 1))),
        out_shape=jax.ShapeDtypeStruct((t, n_out * COL), BF16),
        scratch_shapes=[pltpu.VMEM((tm, d), BF16), pltpu.VMEM((tm, COL), F32)],
        compiler_params=pltpu.CompilerParams(
            dimension_semantics=("parallel", "arbitrary"), vmem_limit_bytes=VMEM_LIMIT),
        name="in_proj",
    )(x, g1, w_in, *tabs, gq, gk, seg, gv, ws, bs)


def _attn_kernel(lamv_ref, q_ref, k_ref, v_ref, g_ref, o_ref,
                 qst_sc, vt_sc, s_sc, mx_sc, p_sc, al_sc, m_sc, acc_sc, *, lambda_init):
    i = pl.program_id(2)
    tq = q_ref.shape[0]
    nk, _, tk = vt_sc.shape
    r = 2 * tq

    @pl.when(i == 0)
    def _():
        ones = jnp.ones((BF16_SUBLANES, tk), BF16)
        for c in range(nk):
            vt_sc[c, 0:ATT_VD, :] = v_ref[c * tk:(c + 1) * tk, :].astype(F32).T.astype(BF16)
            vt_sc[c, ATT_VD:, :] = ones

    qt = q_ref[...].astype(F32).T
    row = lax.broadcasted_iota(jnp.int32, qt.shape, 0)
    zero = jnp.zeros_like(qt)
    qst_sc[:, 0:tq] = jnp.where(row < ATT_HD, qt, zero).astype(BF16)
    qst_sc[:, tq:r] = jnp.where(row >= ATT_HD, qt, zero).astype(BF16)
    m_sc[...] = jnp.full(m_sc.shape, -jnp.inf, F32)
    acc_sc[...] = jnp.zeros(acc_sc.shape, F32)

    n_chunks = tk // KEY_CHUNK

    def trip(score_blk=None, softmax_slot=None, acc_blk=None):
        if softmax_slot is not None:
            m_prev = m_sc[...]
            m_new = jnp.maximum(m_prev, mx_sc[softmax_slot])
            m_sc[...] = m_new
            al_sc[softmax_slot] = jnp.exp2(m_prev - m_new)
        mx = [None]

        def scores_chunk(c):
            j, slot = score_blk
            rows = slice(c * KEY_CHUNK, (c + 1) * KEY_CHUNK)
            off = pl.multiple_of(j * tk + c * KEY_CHUNK, KEY_CHUNK)
            s = jnp.dot(k_ref[pl.ds(off, KEY_CHUNK), :], qst_sc[...],
                        preferred_element_type=F32)
            s_sc[slot, rows, :] = s
            cmx = jnp.max(s, axis=0, keepdims=True)
            mx[0] = cmx if mx[0] is None else jnp.maximum(mx[0], cmx)

        def exp_chunk(c):
            rows = slice(c * KEY_CHUNK, (c + 1) * KEY_CHUNK)
            p_sc[softmax_slot, rows, :] = jnp.exp2(
                s_sc[softmax_slot, rows, :] - m_new).astype(BF16)

        def values():
            j, slot = acc_blk
            acc_sc[...] = al_sc[slot] * acc_sc[...] + jnp.dot(
                vt_sc[j], p_sc[slot], preferred_element_type=F32)

        for step in TRIP_TRACE_ORDER:
            kind, c = step[0], int(step[1:])
            if kind == "V":
                if acc_blk is not None:
                    values()
                continue
            for cc in range(c * n_chunks // 4, (c + 1) * n_chunks // 4):
                if kind == "S" and score_blk is not None:
                    scores_chunk(cc)
                if kind == "E" and softmax_slot is not None:
                    exp_chunk(cc)
        if score_blk is not None:
            mx_sc[score_blk[1]] = mx[0]

    def full_trip(j, parity):
        trip(score_blk=(j + 1, 1 - parity), softmax_slot=parity, acc_blk=(j - 1, 1 - parity))

    trip(score_blk=(0, 0))
    trip(score_blk=(1, 1), softmax_slot=0)

    n_full = nk - 2
    n_loops = n_full // TRIPS_PER_LOOP

    def unrolled(jj, carry):
        j0 = 1 + TRIPS_PER_LOOP * jj
        for u in range(TRIPS_PER_LOOP):
            full_trip(j0 + u, (1 + u) % 2)
        return carry

    lax.fori_loop(0, n_loops, unrolled, 0)
    for j in range(1 + n_loops * TRIPS_PER_LOOP, nk - 1):
        full_trip(j, j % 2)
    trip(softmax_slot=(nk - 1) % 2, acc_blk=(nk - 2, nk % 2))
    trip(acc_blk=(nk - 1, (nk - 1) % 2))

    lv = lamv_ref[...]
    lam = (jnp.exp(jnp.sum(lv[0:1] * lv[1:2], axis=1, keepdims=True))
           - jnp.exp(jnp.sum(lv[2:3] * lv[3:4], axis=1, keepdims=True)) + lambda_init)
    o_all = acc_sc[0:ATT_VD, :] / acc_sc[ATT_VD:ATT_VD + 1, :]
    ot = o_all[:, 0:tq] - lam * o_all[:, tq:r]
    ms = jnp.mean(ot * ot, axis=0, keepdims=True)
    o = (ot * lax.rsqrt(ms + EPS)).T * g_ref[...]
    o_ref[...] = (o * (1.0 - lambda_init)).astype(BF16)


def _attention(proj, lamv, g_sub, *, batch, seq, heads, q_col, k_col, v_col, lambda_init, tq, tk):
    nq = seq // tq
    nk = seq // tk
    assert nk >= 2 and TRIPS_PER_LOOP % 2 == 0, "slots alternate with block parity"
    kern = functools.partial(_attn_kernel, lambda_init=lambda_init)
    return pl.pallas_call(
        kern,
        grid=(batch, heads, nq),
        in_specs=[
            pl.BlockSpec(lamv.shape, lambda b, h, i: (0, 0)),
            pl.BlockSpec((tq, ATT_VD), lambda b, h, i: (b * nq + i, q_col + h)),
            pl.BlockSpec((seq, ATT_VD), lambda b, h, i: (b, k_col + h)),
            pl.BlockSpec((seq, ATT_VD), lambda b, h, i: (b, v_col + h)),
            pl.BlockSpec((1, ATT_VD), lambda b, h, i: (0, 0)),
        ],
        out_specs=pl.BlockSpec((tq, ATT_VD), lambda b, h, i: (b * nq + i, h)),
        out_shape=jax.ShapeDtypeStruct((batch * seq, heads * ATT_VD), BF16),
        scratch_shapes=[
            pltpu.VMEM((ATT_VD, 2 * tq), BF16),
            pltpu.VMEM((nk, ATT_VD + BF16_SUBLANES, tk), BF16),
            pltpu.VMEM((2, tk, 2 * tq), F32),
            pltpu.VMEM((2, 1, 2 * tq), F32),
            pltpu.VMEM((2, tk, 2 * tq), BF16),
            pltpu.VMEM((2, 1, 2 * tq), F32),
            pltpu.VMEM((1, 2 * tq), F32),
            pltpu.VMEM((ATT_VD + BF16_SUBLANES, 2 * tq), F32),
        ],
        compiler_params=pltpu.CompilerParams(
            dimension_semantics=("arbitrary", "arbitrary", "arbitrary"),
            vmem_limit_bytes=VMEM_LIMIT),
        name="diff_attention",
    )(lamv, proj, proj, proj, g_sub)


def _out_proj_kernel(ax_ref, ab_ref, ac_ref, axp_ref, acp_ref, axn_ref, acn_ref, cw_ref,
                     att_ref, oc_ref, x_ref, w_ref, o_ref, mix_sc, *, pos_blocks):
    i = pl.program_id(0)
    j = pl.program_id(1)
    tm, cc = ax_ref.shape

    @pl.when(j == 0)
    def _():
        z = ac_ref[...].astype(F32) * ax_ref[...].astype(F32)
        last = BF16_SUBLANES - 1
        zp = acp_ref[last:last + 1, :].astype(F32) * axp_ref[last:last + 1, :].astype(F32)
        zn = acn_ref[0:1, :].astype(F32) * axn_ref[0:1, :].astype(F32)
        zp = jnp.where(i % pos_blocks == 0, jnp.zeros_like(zp), zp)
        zn = jnp.where(i % pos_blocks == pos_blocks - 1, jnp.zeros_like(zn), zn)
        row = lax.broadcasted_iota(jnp.int32, z.shape, 0)
        z_prev = jnp.where(row == 0, zp, pltpu.roll(z, 1, 0))
        z_next = jnp.where(row == tm - 1, zn, pltpu.roll(z, tm - 1, 0))
        cw = cw_ref[...]
        conv = cw[0:1] * z_prev + cw[1:2] * z + cw[2:3] * z_next
        mix_sc[:, 0:cc] = (ab_ref[...].astype(F32) * conv).astype(BF16)
        aw = att_ref.shape[1]
        mix_sc[:, cc:cc + aw] = att_ref[...]
        mix_sc[:, cc + aw:] = oc_ref[...]

    o_ref[...] = x_ref[...] + jnp.dot(mix_sc[...], w_ref[...], preferred_element_type=F32)


def _out_proj(proj, att, x, conv_w, w_out, *, seq, tm, tn, oc_col):
    t, d = x.shape
    pos_blocks = seq // tm
    hb = tm // BF16_SUBLANES
    n_hb = t // BF16_SUBLANES
    prev = lambda i: jnp.maximum(i * hb - 1, 0)
    nxt = lambda i: jnp.minimum((i + 1) * hb, n_hb - 1)
    kern = functools.partial(_out_proj_kernel, pos_blocks=pos_blocks)
    return pl.pallas_call(
        kern,
        grid=(t // tm, d // tn),
        in_specs=[
            pl.BlockSpec((tm, COL), lambda i, j: (i, 0)),
            pl.BlockSpec((tm, COL), lambda i, j: (i, 1)),
            pl.BlockSpec((tm, COL), lambda i, j: (i, 2)),
            pl.BlockSpec((BF16_SUBLANES, COL), lambda i, j: (prev(i), 0)),
            pl.BlockSpec((BF16_SUBLANES, COL), lambda i, j: (prev(i), 2)),
            pl.BlockSpec((BF16_SUBLANES, COL), lambda i, j: (nxt(i), 0)),
            pl.BlockSpec((BF16_SUBLANES, COL), lambda i, j: (nxt(i), 2)),
            pl.BlockSpec(conv_w.shape, lambda i, j: (0, 0)),
            pl.BlockSpec((tm, att.shape[1]), lambda i, j: (i, 0)),
            pl.BlockSpec((tm, COL), lambda i, j: (i, oc_col)),
            pl.BlockSpec((tm, tn), lambda i, j: (i, j)),
            pl.BlockSpec((w_out.shape[0], tn), lambda i, j: (0, j)),
        ],
        out_specs=pl.BlockSpec((tm, tn), lambda i, j: (i, j)),
        out_shape=jax.ShapeDtypeStruct((t, d), F32),
        scratch_shapes=[pltpu.VMEM((tm, w_out.shape[0]), BF16)],
        compiler_params=pltpu.CompilerParams(
            dimension_semantics=("parallel", "arbitrary"), vmem_limit_bytes=VMEM_LIMIT),
        name="out_proj",
    )(proj, proj, proj, proj, proj, proj, proj, conv_w, att, proj, x, w_out)


def _ffn_kernel(x_ref, g_ref, wu_ref, wd_ref, o_ref, h_sc):
    f = pl.program_id(1)

    @pl.when(f == 0)
    def _():
        x = x_ref[...]
        h_sc[...] = _rms(x, g_ref[...]).astype(BF16)
        o_ref[...] = x

    hid = jnp.dot(h_sc[...], wu_ref[...], preferred_element_type=F32)
    hid = jnp.square(jnp.maximum(hid, 0.0)).astype(BF16)
    o_ref[...] += jnp.dot(hid, wd_ref[...], preferred_element_type=F32)


def _ffn(x, g2, w_up, w_down, *, tm, tf):
    t, d = x.shape
    ff = w_up.shape[1]
    return pl.pallas_call(
        _ffn_kernel,
        grid=(t // tm, ff // tf),
        in_specs=[
            pl.BlockSpec((tm, d), lambda i, f: (i, 0)),
            pl.BlockSpec((1, d), lambda i, f: (0, 0)),
            pl.BlockSpec((d, tf), lambda i, f: (0, f)),
            pl.BlockSpec((tf, d), lambda i, f: (f, 0)),
        ],
        out_specs=pl.BlockSpec((tm, d), lambda i, f: (i, 0)),
        out_shape=jax.ShapeDtypeStruct((t, d), F32),
        scratch_shapes=[pltpu.VMEM((tm, d), BF16)],
        compiler_params=pltpu.CompilerParams(
            dimension_semantics=("parallel", "arbitrary"), vmem_limit_bytes=VMEM_LIMIT),
        name="ffn",
    )(x, g2, w_up, w_down)


def _rope_tables(seq, rot_dim):
    half = rot_dim // 2
    inv = ROPE_THETA ** (-jnp.arange(0, rot_dim, 2, dtype=F32) / rot_dim)
    ang = jnp.arange(seq, dtype=F32)[:, None] * inv[None, :]
    cos, sin = jnp.cos(ang), jnp.sin(ang)
    pad = ATT_HD - rot_dim
    ones = jnp.ones((seq, pad), F32)
    zeros = jnp.zeros((seq, pad), F32)
    zh = jnp.zeros((seq, half), F32)
    rep = LANES // ATT_HD
    cos_t = jnp.tile(jnp.concatenate([cos, cos, ones], axis=1), (1, rep))
    sa_t = jnp.tile(jnp.concatenate([-sin, zh, zeros], axis=1), (1, rep))
    sb_t = jnp.tile(jnp.concatenate([zh, sin, zeros], axis=1), (1, rep))
    return cos_t, sa_t, sb_t


def _prep_layer(l, norm1_g, w_in, conv_w, q_norm_g, k_norm_g, lam_q1, lam_k1, lam_q2, lam_k2,
                subln_g, sgu_norm_g, sgu_w, sgu_b, w_out, norm2_g, w_up, w_down):
    d = w_in.shape[1]
    mlp_heads = sgu_w.shape[1]
    grp = jnp.arange(COL) // ATT_HD
    seg = jnp.where(grp[:, None] == grp[None, :], 1.0 / ATT_HD, 0.0).astype(BF16)
    return dict(
        g1=norm1_g[l].reshape(1, d),
        w_in=w_in[l].astype(BF16),
        conv_w=conv_w[l],
        gq=jnp.tile(q_norm_g[l], COL // ATT_HD).reshape(1, COL),
        gk=jnp.tile(k_norm_g[l], COL // ATT_HD).reshape(1, COL),
        seg=seg,
        lamv=jnp.stack([lam_q1[l], lam_k1[l], lam_q2[l], lam_k2[l]]).astype(F32),
        g_sub=subln_g[l].reshape(1, ATT_VD),
        gv=jnp.tile(sgu_norm_g[l], mlp_heads).reshape(1, mlp_heads * MLP_HD),
        ws=sgu_w[l].astype(BF16),
        bs=jnp.repeat(sgu_b[l].T, MLP_HD, axis=1),
        w_out=w_out[l].astype(BF16),
        g2=norm2_g[l].reshape(1, d),
        w_up=w_up[l].astype(BF16),
        w_down=w_down[l].astype(BF16),
        lambda_init=0.8 - 0.6 * math.exp(-0.3 * l),
    )


def _trunk(x, layers, tm=512, tq=256, tk=512, tn=512, tf=512):
    b, s, d = x.shape
    t = b * s
    heads = (d // 2) // ATT_VD
    tabs = _rope_tables(s, ATT_HD // 4)
    xf = x.reshape(t, d)
    for p in layers:
        proj = _in_proj(xf, p["g1"], p["w_in"], tabs, p["gq"], p["gk"], p["seg"], p["gv"],
                        p["ws"], p["bs"], seq=s, tm=tm)
        att = _attention(proj, p["lamv"], p["g_sub"], batch=b, seq=s, heads=heads,
                         q_col=12, k_col=12 + heads, v_col=12 + 2 * heads,
                         lambda_init=p["lambda_init"], tq=tq, tk=tk)
        x1 = _out_proj(proj, att, xf, p["conv_w"], p["w_out"], seq=s, tm=tm, tn=tn, oc_col=9)
        xf = _ffn(x1, p["g2"], p["w_up"], p["w_down"], tm=tm, tf=tf)
    return xf.reshape(b, s, d)


def kernel(x_prompt, x_sample, norm1_g, w_in, conv_w, q_norm_g, k_norm_g, lam_q1, lam_k1, lam_q2,
           lam_k2, subln_g, sgu_norm_g, sgu_w, sgu_b, w_out, norm2_g, w_up, w_down):
    params = (norm1_g, w_in, conv_w, q_norm_g, k_norm_g, lam_q1, lam_k1, lam_q2, lam_k2,
              subln_g, sgu_norm_g, sgu_w, sgu_b, w_out, norm2_g, w_up, w_down)
    layers = [_prep_layer(l, *params) for l in range(w_in.shape[0])]
    return (_trunk(x_prompt, layers), _trunk(x_sample, layers))
```

```python
import functools
import math

import jax
import jax.numpy as jnp
from jax import lax
from jax.experimental import pallas as pl
from jax.experimental.pallas import tpu as pltpu

F32 = jnp.float32
BF16 = jnp.bfloat16

EPS = 1e-6
ROPE_THETA = 500000.0
CONV_WIDTH = 3
ATT_HD = 64
ATT_VD = 2 * ATT_HD
KEY_CHUNK = 128
TRIPS_PER_LOOP = 4
TRIP_TRACE_ORDER = ("S0", "E0", "S1", "E1", "V0", "S2", "E2", "S3", "E3")
CHUNK = 128
MLP_HD = 128
LANES = 128
MXU_DIM = 256
BF16_SUBLANES = 16
COL = 512
VMEM_LIMIT = 56 * 1024 * 1024
LOG2E = math.log2(math.e)


def _rms(x, gain):
    ms = jnp.mean(x * x, axis=-1, keepdims=True)
    return x * lax.rsqrt(ms + EPS) * gain


def _in_proj_kernel(x_ref, g1_ref, w_ref, cos_ref, sa_ref, sb_ref, gq_ref, gk_ref, seg_ref,
                    gv_ref, ws_ref, bs_ref, o_ref, h_sc, u_sc, *, n_conv, n_qk, n_v):
    j = pl.program_id(1)
    tm = x_ref.shape[0]

    @pl.when(j == 0)
    def _():
        h_sc[...] = _rms(x_ref[...], g1_ref[...]).astype(BF16)

    acc = jnp.dot(h_sc[...], w_ref[...], preferred_element_type=F32)

    q_lo, k_lo = n_conv, n_conv + n_qk
    v_lo, u_at = k_lo + n_qk, k_lo + n_qk + n_v

    def qk_norm_rope(gain):
        sq = (acc * acc).astype(BF16)
        ms = jnp.concatenate(
            [jnp.dot(sq[:, c:c + MXU_DIM], seg_ref[...], preferred_element_type=F32)
             for c in range(0, COL, MXU_DIM)], axis=1)
        y = acc * lax.rsqrt(ms + EPS) * gain
        cos, sa, sb = cos_ref[...], sa_ref[...], sb_ref[...]
        for c in range(COL // LANES):
            yc = y[:, c * LANES:(c + 1) * LANES]
            r = yc * cos + pltpu.roll(yc, LANES - 8, 1) * sa + pltpu.roll(yc, 8, 1) * sb
            o_ref[:, c * LANES:(c + 1) * LANES] = r.astype(BF16)

    @pl.when((j < q_lo) | ((j >= v_lo) & (j < u_at)))
    def _():
        o_ref[...] = acc.astype(BF16)

    @pl.when((j >= q_lo) & (j < k_lo))
    def _():
        qk_norm_rope(gq_ref[...] * (ATT_HD ** -0.5 * LOG2E))

    @pl.when((j >= k_lo) & (j < v_lo))
    def _():
        qk_norm_rope(gk_ref[...])

    @pl.when(j == u_at)
    def _():
        u_sc[...] = jax.nn.gelu(acc)

    @pl.when(j == u_at + 1)
    def _():
        v = jax.nn.gelu(acc)
        for h in range(COL // MLP_HD):
            sl = slice(h * MLP_HD, (h + 1) * MLP_HD)
            vn = _rms(v[:, sl], gv_ref[:, sl]).astype(BF16)
            w = ws_ref[h]
            for n in range(tm // CHUNK):
                rows = slice(n * CHUNK, (n + 1) * CHUNK)
                mixed = jnp.dot(w, vn[rows, :], preferred_element_type=F32) + bs_ref[:, sl]
                o_ref[rows, sl] = (u_sc[rows, sl] * mixed).astype(BF16)


def _in_proj(x, g1, w_in, tabs, gq, gk, seg, gv, ws, bs, *, seq, tm):
    t, d = x.shape
    n_in = w_in.shape[1] // COL
    n_out = n_in - 1
    pos_blocks = seq // tm
    const = lambda i, j: (0, 0)
    kern = functools.partial(_in_proj_kernel, n_conv=3, n_qk=2, n_v=2)
    return pl.pallas_call(
        kern,
        grid=(t // tm, n_in),
        in_specs=[
            pl.BlockSpec((tm, d), lambda i, j: (i, 0)),
            pl.BlockSpec((1, d), const),
            pl.BlockSpec((d, COL), lambda i, j: (0, j)),
            pl.BlockSpec((tm, LANES), lambda i, j: (i % pos_blocks, 0)),
            pl.BlockSpec((tm, LANES), lambda i, j: (i % pos_blocks, 0)),
            pl.BlockSpec((tm, LANES), lambda i, j: (i % pos_blocks, 0)),
            pl.BlockSpec((1, COL), const),
            pl.BlockSpec((1, COL), const),
            pl.BlockSpec((MXU_DIM, MXU_DIM), const),
            pl.BlockSpec((1, COL), const),
            pl.BlockSpec(ws.shape, lambda i, j: (0, 0, 0)),
            pl.BlockSpec((CHUNK, COL), const),
        ],
        out_specs=pl.BlockSpec((tm, COL), lambda i, j: (i, jnp.minimum(j, n_out - 1))),
        out_shape=jax.ShapeDtypeStruct((t, n_out * COL), BF16),
        scratch_shapes=[pltpu.VMEM((tm, d), BF16), pltpu.VMEM((tm, COL), F32)],
        compiler_params=pltpu.CompilerParams(
            dimension_semantics=("parallel", "arbitrary"), vmem_limit_bytes=VMEM_LIMIT),
        name="in_proj",
    )(x, g1, w_in, *tabs, gq, gk, seg, gv, ws, bs)


def _attn_kernel(lamv_ref, q_ref, k_ref, v_ref, g_ref, o_ref,
                 qst_sc, vt_sc, s_sc, mx_sc, p_sc, al_sc, m_sc, acc_sc, *, lambda_init):
    i = pl.program_id(2)
    tq = q_ref.shape[0]
    nk, _, tk = vt_sc.shape
    r = 2 * tq

    @pl.when(i == 0)
    def _():
        ones = jnp.ones((BF16_SUBLANES, tk), BF16)
        for c in range(nk):
            vt_sc[c, 0:ATT_VD, :] = v_ref[c * tk:(c + 1) * tk, :].astype(F32).T.astype(BF16)
            vt_sc[c, ATT_VD:, :] = ones

    qt = q_ref[...].astype(F32).T
    row = lax.broadcasted_iota(jnp.int32, qt.shape, 0)
    zero = jnp.zeros_like(qt)
    qst_sc[:, 0:tq] = jnp.where(row < ATT_HD, qt, zero).astype(BF16)
    qst_sc[:, tq:r] = jnp.where(row >= ATT_HD, qt, zero).astype(BF16)
    m_sc[...] = jnp.full(m_sc.shape, -jnp.inf, F32)
    acc_sc[...] = jnp.zeros(acc_sc.shape, F32)

    n_chunks = tk // KEY_CHUNK

    def trip(score_blk=None, softmax_slot=None, acc_blk=None):
        if softmax_slot is not None:
            m_prev = m_sc[...]
            m_new = jnp.maximum(m_prev, mx_sc[softmax_slot])
            m_sc[...] = m_new
            al_sc[softmax_slot] = jnp.exp2(m_prev - m_new)
        mx = [None]

        def scores_chunk(c):
            j, slot = score_blk
            rows = slice(c * KEY_CHUNK, (c + 1) * KEY_CHUNK)
            off = pl.multiple_of(j * tk + c * KEY_CHUNK, KEY_CHUNK)
            s = jnp.dot(k_ref[pl.ds(off, KEY_CHUNK), :], qst_sc[...],
                        preferred_element_type=F32)
            s_sc[slot, rows, :] = s
            cmx = jnp.max(s, axis=0, keepdims=True)
            mx[0] = cmx if mx[0] is None else jnp.maximum(mx[0], cmx)

        def exp_chunk(c):
            rows = slice(c * KEY_CHUNK, (c + 1) * KEY_CHUNK)
            p_sc[softmax_slot, rows, :] = jnp.exp2(
                s_sc[softmax_slot, rows, :] - m_new).astype(BF16)

        def values():
            j, slot = acc_blk
            acc_sc[...] = al_sc[slot] * acc_sc[...] + jnp.dot(
                vt_sc[j], p_sc[slot], preferred_element_type=F32)

        for step in TRIP_TRACE_ORDER:
            kind, c = step[0], int(step[1:])
            if kind == "V":
                if acc_blk is not None:
                    values()
                continue
            for cc in range(c * n_chunks // 4, (c + 1) * n_chunks // 4):
                if kind == "S" and score_blk is not None:
                    scores_chunk(cc)
                if kind == "E" and softmax_slot is not None:
                    exp_chunk(cc)
        if score_blk is not None:
            mx_sc[score_blk[1]] = mx[0]

    def full_trip(j, parity):
        trip(score_blk=(j + 1, 1 - parity), softmax_slot=parity, acc_blk=(j - 1, 1 - parity))

    trip(score_blk=(0, 0))
    trip(score_blk=(1, 1), softmax_slot=0)

    n_full = nk - 2
    n_loops = n_full // TRIPS_PER_LOOP

    def unrolled(jj, carry):
        j0 = 1 + TRIPS_PER_LOOP * jj
        for u in range(TRIPS_PER_LOOP):
            full_trip(j0 + u, (1 + u) % 2)
        return carry

    lax.fori_loop(0, n_loops, unrolled, 0)
    for j in range(1 + n_loops * TRIPS_PER_LOOP, nk - 1):
        full_trip(j, j % 2)
    trip(softmax_slot=(nk - 1) % 2, acc_blk=(nk - 2, nk % 2))
    trip(acc_blk=(nk - 1, (nk - 1) % 2))

    lv = lamv_ref[...]
    lam = (jnp.exp(jnp.sum(lv[0:1] * lv[1:2], axis=1, keepdims=True))
           - jnp.exp(jnp.sum(lv[2:3] * lv[3:4], axis=1, keepdims=True)) + lambda_init)
    o_all = acc_sc[0:ATT_VD, :] / acc_sc[ATT_VD:ATT_VD + 1, :]
    ot = o_all[:, 0:tq] - lam * o_all[:, tq:r]
    ms = jnp.mean(ot * ot, axis=0, keepdims=True)
    o = (ot * lax.rsqrt(ms + EPS)).T * g_ref[...]
    o_ref[...] = (o * (1.0 - lambda_init)).astype(BF16)


def _attention(proj, lamv, g_sub, *, batch, seq, heads, q_col, k_col, v_col, lambda_init, tq, tk):
    nq = seq // tq
    nk = seq // tk
    assert nk >= 2 and TRIPS_PER_LOOP % 2 == 0, "slots alternate with block parity"
    kern = functools.partial(_attn_kernel, lambda_init=lambda_init)
    return pl.pallas_call(
        kern,
        grid=(batch, heads, nq),
        in_specs=[
            pl.BlockSpec(lamv.shape, lambda b, h, i: (0, 0)),
            pl.BlockSpec((tq, ATT_VD), lambda b, h, i: (b * nq + i, q_col + h)),
            pl.BlockSpec((seq, ATT_VD), lambda b, h, i: (b, k_col + h)),
            pl.BlockSpec((seq, ATT_VD), lambda b, h, i: (b, v_col + h)),
            pl.BlockSpec((1, ATT_VD), lambda b, h, i: (0, 0)),
        ],
        out_specs=pl.BlockSpec((tq, ATT_VD), lambda b, h, i: (b * nq + i, h)),
        out_shape=jax.ShapeDtypeStruct((batch * seq, heads * ATT_VD), BF16),
        scratch_shapes=[
            pltpu.VMEM((ATT_VD, 2 * tq), BF16),
            pltpu.VMEM((nk, ATT_VD + BF16_SUBLANES, tk), BF16),
            pltpu.VMEM((2, tk, 2 * tq), F32),
            pltpu.VMEM((2, 1, 2 * tq), F32),
            pltpu.VMEM((2, tk, 2 * tq), BF16),
            pltpu.VMEM((2, 1, 2 * tq), F32),
            pltpu.VMEM((1, 2 * tq), F32),
            pltpu.VMEM((ATT_VD + BF16_SUBLANES, 2 * tq), F32),
        ],
        compiler_params=pltpu.CompilerParams(
            dimension_semantics=("arbitrary", "arbitrary", "arbitrary"),
            vmem_limit_bytes=VMEM_LIMIT),
        name="diff_attention",
    )(lamv, proj, proj, proj, g_sub)


def _out_proj_kernel(ax_ref, ab_ref, ac_ref, axp_ref, acp_ref, axn_ref, acn_ref, cw_ref,
                     att_ref, oc_ref, x_ref, w_ref, o_ref, mix_sc, *, pos_blocks):
    i = pl.program_id(0)
    j = pl.program_id(1)
    tm, cc = ax_ref.shape

    @pl.when(j == 0)
    def _():
        z = ac_ref[...].astype(F32) * ax_ref[...].astype(F32)
        last = BF16_SUBLANES - 1
        zp = acp_ref[last:last + 1, :].astype(F32) * axp_ref[last:last + 1, :].astype(F32)
        zn = acn_ref[0:1, :].astype(F32) * axn_ref[0:1, :].astype(F32)
        zp = jnp.where(i % pos_blocks == 0, jnp.zeros_like(zp), zp)
        zn = jnp.where(i % pos_blocks == pos_blocks - 1, jnp.zeros_like(zn), zn)
        row = lax.broadcasted_iota(jnp.int32, z.shape, 0)
        z_prev = jnp.where(row == 0, zp, pltpu.roll(z, 1, 0))
        z_next = jnp.where(row == tm - 1, zn, pltpu.roll(z, tm - 1, 0))
        cw = cw_ref[...]
        conv = cw[0:1] * z_prev + cw[1:2] * z + cw[2:3] * z_next
        mix_sc[:, 0:cc] = (ab_ref[...].astype(F32) * conv).astype(BF16)
        aw = att_ref.shape[1]
        mix_sc[:, cc:cc + aw] = att_ref[...]
        mix_sc[:, cc + aw:] = oc_ref[...]

    o_ref[...] = x_ref[...] + jnp.dot(mix_sc[...], w_ref[...], preferred_element_type=F32)


def _out_proj(proj, att, x, conv_w, w_out, *, seq, tm, tn, oc_col):
    t, d = x.shape
    pos_blocks = seq // tm
    hb = tm // BF16_SUBLANES
    n_hb = t // BF16_SUBLANES
    prev = lambda i: jnp.maximum(i * hb - 1, 0)
    nxt = lambda i: jnp.minimum((i + 1) * hb, n_hb - 1)
    kern = functools.partial(_out_proj_kernel, pos_blocks=pos_blocks)
    return pl.pallas_call(
        kern,
        grid=(t // tm, d // tn),
        in_specs=[
            pl.BlockSpec((tm, COL), lambda i, j: (i, 0)),
            pl.BlockSpec((tm, COL), lambda i, j: (i, 1)),
            pl.BlockSpec((tm, COL), lambda i, j: (i, 2)),
            pl.BlockSpec((BF16_SUBLANES, COL), lambda i, j: (prev(i), 0)),
            pl.BlockSpec((BF16_SUBLANES, COL), lambda i, j: (prev(i), 2)),
            pl.BlockSpec((BF16_SUBLANES, COL), lambda i, j: (nxt(i), 0)),
            pl.BlockSpec((BF16_SUBLANES, COL), lambda i, j: (nxt(i), 2)),
            pl.BlockSpec(conv_w.shape, lambda i, j: (0, 0)),
            pl.BlockSpec((tm, att.shape[1]), lambda i, j: (i, 0)),
            pl.BlockSpec((tm, COL), lambda i, j: (i, oc_col)),
            pl.BlockSpec((tm, tn), lambda i, j: (i, j)),
            pl.BlockSpec((w_out.shape[0], tn), lambda i, j: (0, j)),
        ],
        out_specs=pl.BlockSpec((tm, tn), lambda i, j: (i, j)),
        out_shape=jax.ShapeDtypeStruct((t, d), F32),
        scratch_shapes=[pltpu.VMEM((tm, w_out.shape[0]), BF16)],
        compiler_params=pltpu.CompilerParams(
            dimension_semantics=("parallel", "arbitrary"), vmem_limit_bytes=VMEM_LIMIT),
        name="out_proj",
    )(proj, proj, proj, proj, proj, proj, proj, conv_w, att, proj, x, w_out)


def _ffn_kernel(x_ref, g_ref, wu_ref, wd_ref, o_ref, h_sc):
    f = pl.program_id(1)

    @pl.when(f == 0)
    def _():
        x = x_ref[...]
        h_sc[...] = _rms(x, g_ref[...]).astype(BF16)
        o_ref[...] = x

    hid = jnp.dot(h_sc[...], wu_ref[...], preferred_element_type=F32)
    hid = jnp.square(jnp.maximum(hid, 0.0)).astype(BF16)
    o_ref[...] += jnp.dot(hid, wd_ref[...], preferred_element_type=F32)


def _ffn(x, g2, w_up, w_down, *, tm, tf):
    t, d = x.shape
    ff = w_up.shape[1]
    return pl.pallas_call(
        _ffn_kernel,
        grid=(t // tm, ff // tf),
        in_specs=[
            pl.BlockSpec((tm, d), lambda i, f: (i, 0)),
            pl.BlockSpec((1, d), lambda i, f: (0, 0)),
            pl.BlockSpec((d, tf), lambda i, f: (0, f)),
            pl.BlockSpec((tf, d), lambda i, f: (f, 0)),
        ],
        out_specs=pl.BlockSpec((tm, d), lambda i, f: (i, 0)),
        out_shape=jax.ShapeDtypeStruct((t, d), F32),
        scratch_shapes=[pltpu.VMEM((tm, d), BF16)],
        compiler_params=pltpu.CompilerParams(
            dimension_semantics=("parallel", "arbitrary"), vmem_limit_bytes=VMEM_LIMIT),
        name="ffn",
    )(x, g2, w_up, w_down)


def _rope_tables(seq, rot_dim):
    half = rot_dim // 2
    inv = ROPE_THETA ** (-jnp.arange(0, rot_dim, 2, dtype=F32) / rot_dim)
    ang = jnp.arange(seq, dtype=F32)[:, None] * inv[None, :]
    cos, sin = jnp.cos(ang), jnp.sin(ang)
    pad = ATT_HD - rot_dim
    ones = jnp.ones((seq, pad), F32)
    zeros = jnp.zeros((seq, pad), F32)
    zh = jnp.zeros((seq, half), F32)
    rep = LANES // ATT_HD
    cos_t = jnp.tile(jnp.concatenate([cos, cos, ones], axis=1), (1, rep))
    sa_t = jnp.tile(jnp.concatenate([-sin, zh, zeros], axis=1), (1, rep))
    sb_t = jnp.tile(jnp.concatenate([zh, sin, zeros], axis=1), (1, rep))
    return cos_t, sa_t, sb_t


def _prep_layer(l, norm1_g, w_in, conv_w, q_norm_g, k_norm_g, lam_q1, lam_k1, lam_q2, lam_k2,
                subln_g, sgu_norm_g, sgu_w, sgu_b, w_out, norm2_g, w_up, w_down):
    d = w_in.shape[1]
    mlp_heads = sgu_w.shape[1]
    grp = jnp.arange(MXU_DIM) // ATT_HD
    seg = jnp.where(grp[:, None] == grp[None, :], 1.0 / ATT_HD, 0.0).astype(BF16)
    return dict(
        g1=norm1_g[l].reshape(1, d),
        w_in=w_in[l].astype(BF16),
        conv_w=conv_w[l],
        gq=jnp.tile(q_norm_g[l], COL // ATT_HD).reshape(1, COL),
        gk=jnp.tile(k_norm_g[l], COL // ATT_HD).reshape(1, COL),
        seg=seg,
        lamv=jnp.stack([lam_q1[l], lam_k1[l], lam_q2[l], lam_k2[l]]).astype(F32),
        g_sub=subln_g[l].reshape(1, ATT_VD),
        gv=jnp.tile(sgu_norm_g[l], mlp_heads).reshape(1, mlp_heads * MLP_HD),
        ws=sgu_w[l].astype(BF16),
        bs=jnp.repeat(sgu_b[l].T, MLP_HD, axis=1),
        w_out=w_out[l].astype(BF16),
        g2=norm2_g[l].reshape(1, d),
        w_up=w_up[l].astype(BF16),
        w_down=w_down[l].astype(BF16),
        lambda_init=0.8 - 0.6 * math.exp(-0.3 * l),
    )


def _trunk(x, layers, tm=1024, tq=256, tk=512, tn=512, tf=512):
    b, s, d = x.shape
    t = b * s
    heads = (d // 2) // ATT_VD
    tabs = _rope_tables(s, ATT_HD // 4)
    xf = x.reshape(t, d)
    for p in layers:
        proj = _in_proj(xf, p["g1"], p["w_in"], tabs, p["gq"], p["gk"], p["seg"], p["gv"],
                        p["ws"], p["bs"], seq=s, tm=tm)
        att = _attention(proj, p["lamv"], p["g_sub"], batch=b, seq=s, heads=heads,
                         q_col=12, k_col=12 + heads, v_col=12 + 2 * heads,
                         lambda_init=p["lambda_init"], tq=tq, tk=tk)
        x1 = _out_proj(proj, att, xf, p["conv_w"], p["w_out"], seq=s, tm=tm, tn=tn, oc_col=9)
        xf = _ffn(x1, p["g2"], p["w_up"], p["w_down"], tm=tm, tf=tf)
    return xf.reshape(b, s, d)


def kernel(x_prompt, x_sample, norm1_g, w_in, conv_w, q_norm_g, k_norm_g, lam_q1, lam_k1, lam_q2,
           lam_k2, subln_g, sgu_norm_g, sgu_w, sgu_b, w_out, norm2_g, w_up, w_down):
    params = (norm1_g, w_in, conv_w, q_norm_g, k_norm_g, lam_q1, lam_k1, lam_q2, lam_k2,
              subln_g, sgu_norm_g, sgu_w, sgu_b, w_out, norm2_g, w_up, w_down)
    layers = [_prep_layer(l, *params) for l in range(w_in.shape[0])]
    return (_trunk(x_prompt, layers), _trunk(x_sample, layers))
```

```python
import functools
import math

import jax
import jax.numpy as jnp
from jax import lax
from jax.experimental import pallas as pl
from jax.experimental.pallas import tpu as pltpu

F32 = jnp.float32
BF16 = jnp.bfloat16

EPS = 1e-6
ROPE_THETA = 500000.0
CONV_WIDTH = 3
ATT_HD = 64
ATT_VD = 2 * ATT_HD
KEY_CHUNK = 128
TRIPS_PER_LOOP = 4
MAX_LEAD = 64.0
CHUNK = 128
MLP_HD = 128
LANES = 128
MXU_DIM = 256
BF16_SUBLANES = 16
COL = 512
VMEM_LIMIT = 56 * 1024 * 1024
LOG2E = math.log2(math.e)


def _rms(x, gain):
    ms = jnp.mean(x * x, axis=-1, keepdims=True)
    return x * lax.rsqrt(ms + EPS) * gain


def _in_proj_kernel(x_ref, g1_ref, w_ref, cos_ref, sa_ref, sb_ref, gq_ref, gk_ref, seg_ref,
                    gv_ref, ws_ref, bs_ref, o_ref, h_sc, u_sc, *, n_conv, n_qk, n_v):
    j = pl.program_id(1)
    tm = x_ref.shape[0]

    @pl.when(j == 0)
    def _():
        h_sc[...] = _rms(x_ref[...], g1_ref[...]).astype(BF16)

    acc = jnp.dot(h_sc[...], w_ref[...], preferred_element_type=F32)

    q_lo, k_lo = n_conv, n_conv + n_qk
    v_lo, u_at = k_lo + n_qk, k_lo + n_qk + n_v

    def qk_norm_rope(gain):
        sq = (acc * acc).astype(BF16)
        ms = jnp.concatenate(
            [jnp.dot(sq[:, c:c + MXU_DIM], seg_ref[...], preferred_element_type=F32)
             for c in range(0, COL, MXU_DIM)], axis=1)
        y = acc * lax.rsqrt(ms + EPS) * gain
        cos, sa, sb = cos_ref[...], sa_ref[...], sb_ref[...]
        for c in range(COL // LANES):
            yc = y[:, c * LANES:(c + 1) * LANES]
            r = yc * cos + pltpu.roll(yc, LANES - 8, 1) * sa + pltpu.roll(yc, 8, 1) * sb
            o_ref[:, c * LANES:(c + 1) * LANES] = r.astype(BF16)

    @pl.when((j < q_lo) | ((j >= v_lo) & (j < u_at)))
    def _():
        o_ref[...] = acc.astype(BF16)

    @pl.when((j >= q_lo) & (j < k_lo))
    def _():
        qk_norm_rope(gq_ref[...] * (ATT_HD ** -0.5 * LOG2E))

    @pl.when((j >= k_lo) & (j < v_lo))
    def _():
        qk_norm_rope(gk_ref[...])

    @pl.when(j == u_at)
    def _():
        u_sc[...] = jax.nn.gelu(acc)

    @pl.when(j == u_at + 1)
    def _():
        v = jax.nn.gelu(acc)
        for h in range(COL // MLP_HD):
            sl = slice(h * MLP_HD, (h + 1) * MLP_HD)
            vn = _rms(v[:, sl], gv_ref[:, sl]).astype(BF16)
            w = ws_ref[h]
            for n in range(tm // CHUNK):
                rows = slice(n * CHUNK, (n + 1) * CHUNK)
                mixed = jnp.dot(w, vn[rows, :], preferred_element_type=F32) + bs_ref[:, sl]
                o_ref[rows, sl] = (u_sc[rows, sl] * mixed).astype(BF16)


def _in_proj(x, g1, w_in, tabs, gq, gk, seg, gv, ws, bs, *, seq, tm):
    t, d = x.shape
    n_in = w_in.shape[1] // COL
    n_out = n_in - 1
    pos_blocks = seq // tm
    const = lambda i, j: (0, 0)
    kern = functools.partial(_in_proj_kernel, n_conv=3, n_qk=2, n_v=2)
    return pl.pallas_call(
        kern,
        grid=(t // tm, n_in),
        in_specs=[
            pl.BlockSpec((tm, d), lambda i, j: (i, 0)),
            pl.BlockSpec((1, d), const),
            pl.BlockSpec((d, COL), lambda i, j: (0, j)),
            pl.BlockSpec((tm, LANES), lambda i, j: (i % pos_blocks, 0)),
            pl.BlockSpec((tm, LANES), lambda i, j: (i % pos_blocks, 0)),
            pl.BlockSpec((tm, LANES), lambda i, j: (i % pos_blocks, 0)),
            pl.BlockSpec((1, COL), const),
            pl.BlockSpec((1, COL), const),
            pl.BlockSpec((MXU_DIM, MXU_DIM), const),
            pl.BlockSpec((1, COL), const),
            pl.BlockSpec(ws.shape, lambda i, j: (0, 0, 0)),
            pl.BlockSpec((CHUNK, COL), const),
        ],
        out_specs=pl.BlockSpec((tm, COL), lambda i, j: (i, jnp.minimum(j, n_out - 1))),
        out_shape=jax.ShapeDtypeStruct((t, n_out * COL), BF16),
        scratch_shapes=[pltpu.VMEM((tm, d), BF16), pltpu.VMEM((tm, COL), F32)],
        compiler_params=pltpu.CompilerParams(
            dimension_semantics=("parallel", "arbitrary"), vmem_limit_bytes=VMEM_LIMIT),
        name="in_proj",
    )(x, g1, w_in, *tabs, gq, gk, seg, gv, ws, bs)


def _attn_kernel(lamv_ref, q_ref, k_ref, v_ref, g_ref, o_ref,
                 qst_sc, vt_sc, s_sc, p_sc, al_sc, m_sc, lead_sc, acc_sc, *, lambda_init):
    i = pl.program_id(2)
    tq = q_ref.shape[0]
    nk, _, tk = vt_sc.shape
    r = 2 * tq

    @pl.when(i == 0)
    def _():
        ones = jnp.ones((BF16_SUBLANES, tk), BF16)
        for c in range(nk):
            vt_sc[c, 0:ATT_VD, :] = v_ref[c * tk:(c + 1) * tk, :].astype(F32).T.astype(BF16)
            vt_sc[c, ATT_VD:, :] = ones

    qt = q_ref[...].astype(F32).T
    row = lax.broadcasted_iota(jnp.int32, qt.shape, 0)
    zero = jnp.zeros_like(qt)
    qst_sc[:, 0:tq] = jnp.where(row < ATT_HD, qt, zero).astype(BF16)
    qst_sc[:, tq:r] = jnp.where(row >= ATT_HD, qt, zero).astype(BF16)
    n_chunks = tk // KEY_CHUNK

    def score_chunk(j, c):
        off = pl.multiple_of(j * tk + c * KEY_CHUNK, KEY_CHUNK)
        return jnp.dot(k_ref[pl.ds(off, KEY_CHUNK), :], qst_sc[...],
                       preferred_element_type=F32)

    def block_max(j):
        bm = None
        for c in range(n_chunks):
            cmx = jnp.max(score_chunk(j, c), axis=0, keepdims=True)
            bm = cmx if bm is None else jnp.maximum(bm, cmx)
        return bm

    def add_values(j, slot):
        acc_sc[...] = al_sc[slot] * (acc_sc[...] + jnp.dot(
            vt_sc[j], p_sc[slot], preferred_element_type=F32))

    m_sc[...] = block_max(0)
    lead_sc[...] = jnp.zeros(lead_sc.shape, F32)
    acc_sc[...] = jnp.zeros(acc_sc.shape, F32)

    def trip(j, slot, prev=None):
        m_prev = m_sc[...]
        bm = None
        for c in range(n_chunks):
            rows = slice(c * KEY_CHUNK, (c + 1) * KEY_CHUNK)
            s = score_chunk(j, c)
            p_sc[slot, rows, :] = jnp.exp2(s - m_prev).astype(BF16)
            cmx = jnp.max(s, axis=0, keepdims=True)
            bm = cmx if bm is None else jnp.maximum(bm, cmx)
            if prev is not None and c == max(n_chunks // 2 - 1, 0):
                add_values(*prev)
        m_new = jnp.maximum(m_prev, bm)
        m_sc[...] = m_new
        al_sc[slot] = jnp.exp2(m_prev - m_new)
        lead_sc[...] = jnp.maximum(lead_sc[...], bm - m_prev)

    trip(0, 0)
    n_loops = (nk - 1) // TRIPS_PER_LOOP

    def unrolled(jj, carry):
        j0 = 1 + TRIPS_PER_LOOP * jj
        for u in range(TRIPS_PER_LOOP):
            trip(j0 + u, (1 + u) % 2, prev=(j0 + u - 1, u % 2))
        return carry

    lax.fori_loop(0, n_loops, unrolled, 0)
    for j in range(1 + n_loops * TRIPS_PER_LOOP, nk):
        trip(j, j % 2, prev=(j - 1, (j - 1) % 2))
    add_values(nk - 1, (nk - 1) % 2)

    @pl.when(jnp.max(lead_sc[...]) > MAX_LEAD)
    def _():
        m_sc[...] = jnp.full(m_sc.shape, -jnp.inf, F32)
        acc_sc[...] = jnp.zeros(acc_sc.shape, F32)

        def block(j, carry):
            m_prev = m_sc[...]
            for c in range(n_chunks):
                s_sc[c * KEY_CHUNK:(c + 1) * KEY_CHUNK, :] = score_chunk(j, c)
            s = s_sc[...]
            m_new = jnp.maximum(m_prev, jnp.max(s, axis=0, keepdims=True))
            m_sc[...] = m_new
            acc_sc[...] = jnp.exp2(m_prev - m_new) * acc_sc[...] + jnp.dot(
                vt_sc[j], jnp.exp2(s - m_new).astype(BF16), preferred_element_type=F32)
            return carry

        lax.fori_loop(0, nk, block, 0)

    lv = lamv_ref[...]
    lam = (jnp.exp(jnp.sum(lv[0:1] * lv[1:2], axis=1, keepdims=True))
           - jnp.exp(jnp.sum(lv[2:3] * lv[3:4], axis=1, keepdims=True)) + lambda_init)
    o_all = acc_sc[0:ATT_VD, :] / acc_sc[ATT_VD:ATT_VD + 1, :]
    ot = o_all[:, 0:tq] - lam * o_all[:, tq:r]
    ms = jnp.mean(ot * ot, axis=0, keepdims=True)
    o = (ot * lax.rsqrt(ms + EPS)).T * g_ref[...]
    o_ref[...] = (o * (1.0 - lambda_init)).astype(BF16)


def _attention(proj, lamv, g_sub, *, batch, seq, heads, q_col, k_col, v_col, lambda_init, tq, tk):
    nq = seq // tq
    nk = seq // tk
    assert nk >= 2 and TRIPS_PER_LOOP % 2 == 0, "slots alternate with block parity"
    kern = functools.partial(_attn_kernel, lambda_init=lambda_init)
    return pl.pallas_call(
        kern,
        grid=(batch, heads, nq),
        in_specs=[
            pl.BlockSpec(lamv.shape, lambda b, h, i: (0, 0)),
            pl.BlockSpec((tq, ATT_VD), lambda b, h, i: (b * nq + i, q_col + h)),
            pl.BlockSpec((seq, ATT_VD), lambda b, h, i: (b, k_col + h)),
            pl.BlockSpec((seq, ATT_VD), lambda b, h, i: (b, v_col + h)),
            pl.BlockSpec((1, ATT_VD), lambda b, h, i: (0, 0)),
        ],
        out_specs=pl.BlockSpec((tq, ATT_VD), lambda b, h, i: (b * nq + i, h)),
        out_shape=jax.ShapeDtypeStruct((batch * seq, heads * ATT_VD), BF16),
        scratch_shapes=[
            pltpu.VMEM((ATT_VD, 2 * tq), BF16),
            pltpu.VMEM((nk, ATT_VD + BF16_SUBLANES, tk), BF16),
            pltpu.VMEM((tk, 2 * tq), F32),
            pltpu.VMEM((2, tk, 2 * tq), BF16),
            pltpu.VMEM((2, 1, 2 * tq), F32),
            pltpu.VMEM((1, 2 * tq), F32),
            pltpu.VMEM((1, 2 * tq), F32),
            pltpu.VMEM((ATT_VD + BF16_SUBLANES, 2 * tq), F32),
        ],
        compiler_params=pltpu.CompilerParams(
            dimension_semantics=("arbitrary", "arbitrary", "arbitrary"),
            vmem_limit_bytes=VMEM_LIMIT),
        name="diff_attention",
    )(lamv, proj, proj, proj, g_sub)


def _out_proj_kernel(ax_ref, ab_ref, ac_ref, axp_ref, acp_ref, axn_ref, acn_ref, cw_ref,
                     att_ref, oc_ref, x_ref, w_ref, o_ref, mix_sc, *, pos_blocks):
    i = pl.program_id(0)
    j = pl.program_id(1)
    tm, cc = ax_ref.shape

    @pl.when(j == 0)
    def _():
        z = ac_ref[...].astype(F32) * ax_ref[...].astype(F32)
        last = BF16_SUBLANES - 1
        zp = acp_ref[last:last + 1, :].astype(F32) * axp_ref[last:last + 1, :].astype(F32)
        zn = acn_ref[0:1, :].astype(F32) * axn_ref[0:1, :].astype(F32)
        zp = jnp.where(i % pos_blocks == 0, jnp.zeros_like(zp), zp)
        zn = jnp.where(i % pos_blocks == pos_blocks - 1, jnp.zeros_like(zn), zn)
        row = lax.broadcasted_iota(jnp.int32, z.shape, 0)
        z_prev = jnp.where(row == 0, zp, pltpu.roll(z, 1, 0))
        z_next = jnp.where(row == tm - 1, zn, pltpu.roll(z, tm - 1, 0))
        cw = cw_ref[...]
        conv = cw[0:1] * z_prev + cw[1:2] * z + cw[2:3] * z_next
        mix_sc[:, 0:cc] = (ab_ref[...].astype(F32) * conv).astype(BF16)
        aw = att_ref.shape[1]
        mix_sc[:, cc:cc + aw] = att_ref[...]
        mix_sc[:, cc + aw:] = oc_ref[...]

    o_ref[...] = x_ref[...] + jnp.dot(mix_sc[...], w_ref[...], preferred_element_type=F32)


def _out_proj(proj, att, x, conv_w, w_out, *, seq, tm, tn, oc_col):
    t, d = x.shape
    pos_blocks = seq // tm
    hb = tm // BF16_SUBLANES
    n_hb = t // BF16_SUBLANES
    prev = lambda i: jnp.maximum(i * hb - 1, 0)
    nxt = lambda i: jnp.minimum((i + 1) * hb, n_hb - 1)
    kern = functools.partial(_out_proj_kernel, pos_blocks=pos_blocks)
    return pl.pallas_call(
        kern,
        grid=(t // tm, d // tn),
        in_specs=[
            pl.BlockSpec((tm, COL), lambda i, j: (i, 0)),
            pl.BlockSpec((tm, COL), lambda i, j: (i, 1)),
            pl.BlockSpec((tm, COL), lambda i, j: (i, 2)),
            pl.BlockSpec((BF16_SUBLANES, COL), lambda i, j: (prev(i), 0)),
            pl.BlockSpec((BF16_SUBLANES, COL), lambda i, j: (prev(i), 2)),
            pl.BlockSpec((BF16_SUBLANES, COL), lambda i, j: (nxt(i), 0)),
            pl.BlockSpec((BF16_SUBLANES, COL), lambda i, j: (nxt(i), 2)),
            pl.BlockSpec(conv_w.shape, lambda i, j: (0, 0)),
            pl.BlockSpec((tm, att.shape[1]), lambda i, j: (i, 0)),
            pl.BlockSpec((tm, COL), lambda i, j: (i, oc_col)),
            pl.BlockSpec((tm, tn), lambda i, j: (i, j)),
            pl.BlockSpec((w_out.shape[0], tn), lambda i, j: (0, j)),
        ],
        out_specs=pl.BlockSpec((tm, tn), lambda i, j: (i, j)),
        out_shape=jax.ShapeDtypeStruct((t, d), F32),
        scratch_shapes=[pltpu.VMEM((tm, w_out.shape[0]), BF16)],
        compiler_params=pltpu.CompilerParams(
            dimension_semantics=("parallel", "arbitrary"), vmem_limit_bytes=VMEM_LIMIT),
        name="out_proj",
    )(proj, proj, proj, proj, proj, proj, proj, conv_w, att, proj, x, w_out)


def _ffn_kernel(x_ref, g_ref, wu_ref, wd_ref, o_ref, h_sc):
    f = pl.program_id(1)

    @pl.when(f == 0)
    def _():
        x = x_ref[...]
        h_sc[...] = _rms(x, g_ref[...]).astype(BF16)
        o_ref[...] = x

    hid = jnp.dot(h_sc[...], wu_ref[...], preferred_element_type=F32)
    hid = jnp.square(jnp.maximum(hid, 0.0)).astype(BF16)
    o_ref[...] += jnp.dot(hid, wd_ref[...], preferred_element_type=F32)


def _ffn(x, g2, w_up, w_down, *, tm, tf):
    t, d = x.shape
    ff = w_up.shape[1]
    return pl.pallas_call(
        _ffn_kernel,
        grid=(t // tm, ff // tf),
        in_specs=[
            pl.BlockSpec((tm, d), lambda i, f: (i, 0)),
            pl.BlockSpec((1, d), lambda i, f: (0, 0)),
            pl.BlockSpec((d, tf), lambda i, f: (0, f)),
            pl.BlockSpec((tf, d), lambda i, f: (f, 0)),
        ],
        out_specs=pl.BlockSpec((tm, d), lambda i, f: (i, 0)),
        out_shape=jax.ShapeDtypeStruct((t, d), F32),
        scratch_shapes=[pltpu.VMEM((tm, d), BF16)],
        compiler_params=pltpu.CompilerParams(
            dimension_semantics=("parallel", "arbitrary"), vmem_limit_bytes=VMEM_LIMIT),
        name="ffn",
    )(x, g2, w_up, w_down)


def _rope_tables(seq, rot_dim):
    half = rot_dim // 2
    inv = ROPE_THETA ** (-jnp.arange(0, rot_dim, 2, dtype=F32) / rot_dim)
    ang = jnp.arange(seq, dtype=F32)[:, None] * inv[None, :]
    cos, sin = jnp.cos(ang), jnp.sin(ang)
    pad = ATT_HD - rot_dim
    ones = jnp.ones((seq, pad), F32)
    zeros = jnp.zeros((seq, pad), F32)
    zh = jnp.zeros((seq, half), F32)
    rep = LANES // ATT_HD
    cos_t = jnp.tile(jnp.concatenate([cos, cos, ones], axis=1), (1, rep))
    sa_t = jnp.tile(jnp.concatenate([-sin, zh, zeros], axis=1), (1, rep))
    sb_t = jnp.tile(jnp.concatenate([zh, sin, zeros], axis=1), (1, rep))
    return cos_t, sa_t, sb_t


def _prep_layer(l, norm1_g, w_in, conv_w, q_norm_g, k_norm_g, lam_q1, lam_k1, lam_q2, lam_k2,
                subln_g, sgu_norm_g, sgu_w, sgu_b, w_out, norm2_g, w_up, w_down):
    d = w_in.shape[1]
    mlp_heads = sgu_w.shape[1]
    grp = jnp.arange(MXU_DIM) // ATT_HD
    seg = jnp.where(grp[:, None] == grp[None, :], 1.0 / ATT_HD, 0.0).astype(BF16)
    return dict(
        g1=norm1_g[l].reshape(1, d),
        w_in=w_in[l].astype(BF16),
        conv_w=conv_w[l],
        gq=jnp.tile(q_norm_g[l], COL // ATT_HD).reshape(1, COL),
        gk=jnp.tile(k_norm_g[l], COL // ATT_HD).reshape(1, COL),
        seg=seg,
        lamv=jnp.stack([lam_q1[l], lam_k1[l], lam_q2[l], lam_k2[l]]).astype(F32),
        g_sub=subln_g[l].reshape(1, ATT_VD),
        gv=jnp.tile(sgu_norm_g[l], mlp_heads).reshape(1, mlp_heads * MLP_HD),
        ws=sgu_w[l].astype(BF16),
        bs=jnp.repeat(sgu_b[l].T, MLP_HD, axis=1),
        w_out=w_out[l].astype(BF16),
        g2=norm2_g[l].reshape(1, d),
        w_up=w_up[l].astype(BF16),
        w_down=w_down[l].astype(BF16),
        lambda_init=0.8 - 0.6 * math.exp(-0.3 * l),
    )


def _trunk(x, layers, tm=1024, tq=256, tk=512, tn=512, tf=512):
    b, s, d = x.shape
    t = b * s
    heads = (d // 2) // ATT_VD
    tabs = _rope_tables(s, ATT_HD // 4)
    xf = x.reshape(t, d)
    for p in layers:
        proj = _in_proj(xf, p["g1"], p["w_in"], tabs, p["gq"], p["gk"], p["seg"], p["gv"],
                        p["ws"], p["bs"], seq=s, tm=tm)
        att = _attention(proj, p["lamv"], p["g_sub"], batch=b, seq=s, heads=heads,
                         q_col=12, k_col=12 + heads, v_col=12 + 2 * heads,
                         lambda_init=p["lambda_init"], tq=tq, tk=tk)
        x1 = _out_proj(proj, att, xf, p["conv_w"], p["w_out"], seq=s, tm=tm, tn=tn, oc_col=9)
        xf = _ffn(x1, p["g2"], p["w_up"], p["w_down"], tm=tm, tf=tf)
    return xf.reshape(b, s, d)


def kernel(x_prompt, x_sample, norm1_g, w_in, conv_w, q_norm_g, k_norm_g, lam_q1, lam_k1, lam_q2,
           lam_k2, subln_g, sgu_norm_g, sgu_w, sgu_b, w_out, norm2_g, w_up, w_down):
    params = (norm1_g, w_in, conv_w, q_norm_g, k_norm_g, lam_q1, lam_k1, lam_q2, lam_k2,
              subln_g, sgu_norm_g, sgu_w, sgu_b, w_out, norm2_g, w_up, w_down)
    layers = [_prep_layer(l, *params) for l in range(w_in.shape[0])]
    return (_trunk(x_prompt, layers), _trunk(x_sample, layers))
```

```python
import functools
import math

import jax
import jax.numpy as jnp
from jax import lax
from jax.experimental import pallas as pl
from jax.experimental.pallas import tpu as pltpu

F32 = jnp.float32
BF16 = jnp.bfloat16

EPS = 1e-6
ROPE_THETA = 500000.0
CONV_WIDTH = 3
ATT_HD = 64
ATT_VD = 2 * ATT_HD
KEY_CHUNK = 128
TRIPS_PER_LOOP = 8
MAX_LEAD = 64.0
CHUNK = 128
MLP_HD = 128
LANES = 128
MXU_DIM = 256
BF16_SUBLANES = 16
COL = 512
VMEM_LIMIT = 56 * 1024 * 1024
LOG2E = math.log2(math.e)


def _rms(x, gain):
    ms = jnp.mean(x * x, axis=-1, keepdims=True)
    return x * lax.rsqrt(ms + EPS) * gain


def _in_proj_kernel(x_ref, g1_ref, w_ref, cos_ref, sa_ref, sb_ref, gq_ref, gk_ref, seg_ref,
                    gv_ref, ws_ref, bs_ref, o_ref, h_sc, u_sc, *, n_conv, n_qk, n_v):
    j = pl.program_id(1)
    tm = x_ref.shape[0]

    @pl.when(j == 0)
    def _():
        h_sc[...] = _rms(x_ref[...], g1_ref[...]).astype(BF16)

    acc = jnp.dot(h_sc[...], w_ref[...], preferred_element_type=F32)

    q_lo, k_lo = n_conv, n_conv + n_qk
    v_lo, u_at = k_lo + n_qk, k_lo + n_qk + n_v

    def qk_norm_rope(gain):
        sq = (acc * acc).astype(BF16)
        ms = jnp.concatenate(
            [jnp.dot(sq[:, c:c + MXU_DIM], seg_ref[...], preferred_element_type=F32)
             for c in range(0, COL, MXU_DIM)], axis=1)
        y = acc * lax.rsqrt(ms + EPS) * gain
        cos, sa, sb = cos_ref[...], sa_ref[...], sb_ref[...]
        for c in range(COL // LANES):
            yc = y[:, c * LANES:(c + 1) * LANES]
            r = yc * cos + pltpu.roll(yc, LANES - 8, 1) * sa + pltpu.roll(yc, 8, 1) * sb
            o_ref[:, c * LANES:(c + 1) * LANES] = r.astype(BF16)

    @pl.when((j < q_lo) | ((j >= v_lo) & (j < u_at)))
    def _():
        o_ref[...] = acc.astype(BF16)

    @pl.when((j >= q_lo) & (j < k_lo))
    def _():
        qk_norm_rope(gq_ref[...] * (ATT_HD ** -0.5 * LOG2E))

    @pl.when((j >= k_lo) & (j < v_lo))
    def _():
        qk_norm_rope(gk_ref[...])

    @pl.when(j == u_at)
    def _():
        u_sc[...] = jax.nn.gelu(acc)

    @pl.when(j == u_at + 1)
    def _():
        v = jax.nn.gelu(acc)
        for h in range(COL // MLP_HD):
            sl = slice(h * MLP_HD, (h + 1) * MLP_HD)
            vn = _rms(v[:, sl], gv_ref[:, sl]).astype(BF16)
            w = ws_ref[h]
            for n in range(tm // CHUNK):
                rows = slice(n * CHUNK, (n + 1) * CHUNK)
                mixed = jnp.dot(w, vn[rows, :], preferred_element_type=F32) + bs_ref[:, sl]
                o_ref[rows, sl] = (u_sc[rows, sl] * mixed).astype(BF16)


def _in_proj(x, g1, w_in, tabs, gq, gk, seg, gv, ws, bs, *, seq, tm):
    t, d = x.shape
    n_in = w_in.shape[1] // COL
    n_out = n_in - 1
    pos_blocks = seq // tm
    const = lambda i, j: (0, 0)
    kern = functools.partial(_in_proj_kernel, n_conv=3, n_qk=2, n_v=2)
    return pl.pallas_call(
        kern,
        grid=(t // tm, n_in),
        in_specs=[
            pl.BlockSpec((tm, d), lambda i, j: (i, 0)),
            pl.BlockSpec((1, d), const),
            pl.BlockSpec((d, COL), lambda i, j: (0, j)),
            pl.BlockSpec((tm, LANES), lambda i, j: (i % pos_blocks, 0)),
            pl.BlockSpec((tm, LANES), lambda i, j: (i % pos_blocks, 0)),
            pl.BlockSpec((tm, LANES), lambda i, j: (i % pos_blocks, 0)),
            pl.BlockSpec((1, COL), const),
            pl.BlockSpec((1, COL), const),
            pl.BlockSpec((MXU_DIM, MXU_DIM), const),
            pl.BlockSpec((1, COL), const),
            pl.BlockSpec(ws.shape, lambda i, j: (0, 0, 0)),
            pl.BlockSpec((CHUNK, COL), const),
        ],
        out_specs=pl.BlockSpec((tm, COL), lambda i, j: (i, jnp.minimum(j, n_out - 1))),
        out_shape=jax.ShapeDtypeStruct((t, n_out * COL), BF16),
        scratch_shapes=[pltpu.VMEM((tm, d), BF16), pltpu.VMEM((tm, COL), F32)],
        compiler_params=pltpu.CompilerParams(
            dimension_semantics=("parallel", "arbitrary"), vmem_limit_bytes=VMEM_LIMIT),
        name="in_proj",
    )(x, g1, w_in, *tabs, gq, gk, seg, gv, ws, bs)


def _attn_kernel(lamv_ref, q_ref, k_ref, v_ref, g_ref, o_ref,
                 qst_sc, vt_sc, s_sc, p_sc, al_sc, m_sc, lead_sc, acc_sc, *, lambda_init):
    i = pl.program_id(2)
    tq = q_ref.shape[0]
    nk, _, tk = vt_sc.shape
    r = 2 * tq

    @pl.when(i == 0)
    def _():
        ones = jnp.ones((BF16_SUBLANES, tk), BF16)
        for c in range(nk):
            vt_sc[c, 0:ATT_VD, :] = v_ref[c * tk:(c + 1) * tk, :].astype(F32).T.astype(BF16)
            vt_sc[c, ATT_VD:, :] = ones

    qt = q_ref[...].astype(F32).T
    row = lax.broadcasted_iota(jnp.int32, qt.shape, 0)
    zero = jnp.zeros_like(qt)
    qst_sc[:, 0:tq] = jnp.where(row < ATT_HD, qt, zero).astype(BF16)
    qst_sc[:, tq:r] = jnp.where(row >= ATT_HD, qt, zero).astype(BF16)
    n_chunks = tk // KEY_CHUNK

    def score_chunk(j, c):
        off = pl.multiple_of(j * tk + c * KEY_CHUNK, KEY_CHUNK)
        return jnp.dot(k_ref[pl.ds(off, KEY_CHUNK), :], qst_sc[...],
                       preferred_element_type=F32)

    def add_values(j, slot):
        acc_sc[...] = al_sc[slot] * (acc_sc[...] + jnp.dot(
            vt_sc[j], p_sc[slot], preferred_element_type=F32))

    m_sc[...] = jnp.max(score_chunk(0, 0), axis=0, keepdims=True)
    lead_sc[...] = jnp.zeros(lead_sc.shape, F32)
    acc_sc[...] = jnp.zeros(acc_sc.shape, F32)
    p_sc[1] = jnp.zeros(p_sc.shape[1:], BF16)
    al_sc[1] = jnp.ones(al_sc.shape[1:], F32)

    def trip(j, slot):
        m_prev = m_sc[...]
        bm = None
        for c in range(n_chunks):
            rows = slice(c * KEY_CHUNK, (c + 1) * KEY_CHUNK)
            s = score_chunk(j, c)
            p_sc[slot, rows, :] = jnp.exp2(s - m_prev).astype(BF16)
            cmx = jnp.max(s, axis=0, keepdims=True)
            bm = cmx if bm is None else jnp.maximum(bm, cmx)
            if c == max(n_chunks // 2 - 1, 0):
                add_values(max(j - 1, 0) if isinstance(j, int) else jnp.maximum(j - 1, 0),
                           1 - slot)
        m_new = jnp.maximum(m_prev, bm)
        m_sc[...] = m_new
        al_sc[slot] = jnp.exp2(m_prev - m_new)
        lead_sc[...] = jnp.maximum(lead_sc[...], bm - m_prev)

    n_loops = nk // TRIPS_PER_LOOP

    def unrolled(jj, carry):
        j0 = TRIPS_PER_LOOP * jj
        for u in range(TRIPS_PER_LOOP):
            trip(j0 + u, u % 2)
        return carry

    lax.fori_loop(0, n_loops, unrolled, 0)
    for j in range(n_loops * TRIPS_PER_LOOP, nk):
        trip(j, j % 2)
    add_values(nk - 1, (nk - 1) % 2)

    @pl.when(jnp.max(lead_sc[...]) > MAX_LEAD)
    def _():
        m_sc[...] = jnp.full(m_sc.shape, -jnp.inf, F32)
        acc_sc[...] = jnp.zeros(acc_sc.shape, F32)

        def block(j, carry):
            m_prev = m_sc[...]
            for c in range(n_chunks):
                s_sc[c * KEY_CHUNK:(c + 1) * KEY_CHUNK, :] = score_chunk(j, c)
            s = s_sc[...]
            m_new = jnp.maximum(m_prev, jnp.max(s, axis=0, keepdims=True))
            m_sc[...] = m_new
            acc_sc[...] = jnp.exp2(m_prev - m_new) * acc_sc[...] + jnp.dot(
                vt_sc[j], jnp.exp2(s - m_new).astype(BF16), preferred_element_type=F32)
            return carry

        lax.fori_loop(0, nk, block, 0)

    lv = lamv_ref[...]
    lam = (jnp.exp(jnp.sum(lv[0:1] * lv[1:2], axis=1, keepdims=True))
           - jnp.exp(jnp.sum(lv[2:3] * lv[3:4], axis=1, keepdims=True)) + lambda_init)
    o_all = acc_sc[0:ATT_VD, :] / acc_sc[ATT_VD:ATT_VD + 1, :]
    ot = o_all[:, 0:tq] - lam * o_all[:, tq:r]
    ms = jnp.mean(ot * ot, axis=0, keepdims=True)
    o = (ot * lax.rsqrt(ms + EPS)).T * g_ref[...]
    o_ref[...] = (o * (1.0 - lambda_init)).astype(BF16)


def _attention(proj, lamv, g_sub, *, batch, seq, heads, q_col, k_col, v_col, lambda_init, tq, tk):
    nq = seq // tq
    nk = seq // tk
    assert nk >= 2 and TRIPS_PER_LOOP % 2 == 0, "slots alternate with block parity"
    kern = functools.partial(_attn_kernel, lambda_init=lambda_init)
    return pl.pallas_call(
        kern,
        grid=(batch, heads, nq),
        in_specs=[
            pl.BlockSpec(lamv.shape, lambda b, h, i: (0, 0)),
            pl.BlockSpec((tq, ATT_VD), lambda b, h, i: (b * nq + i, q_col + h)),
            pl.BlockSpec((seq, ATT_VD), lambda b, h, i: (b, k_col + h)),
            pl.BlockSpec((seq, ATT_VD), lambda b, h, i: (b, v_col + h)),
            pl.BlockSpec((1, ATT_VD), lambda b, h, i: (0, 0)),
        ],
        out_specs=pl.BlockSpec((tq, ATT_VD), lambda b, h, i: (b * nq + i, h)),
        out_shape=jax.ShapeDtypeStruct((batch * seq, heads * ATT_VD), BF16),
        scratch_shapes=[
            pltpu.VMEM((ATT_VD, 2 * tq), BF16),
            pltpu.VMEM((nk, ATT_VD + BF16_SUBLANES, tk), BF16),
            pltpu.VMEM((tk, 2 * tq), F32),
            pltpu.VMEM((2, tk, 2 * tq), BF16),
            pltpu.VMEM((2, 1, 2 * tq), F32),
            pltpu.VMEM((1, 2 * tq), F32),
            pltpu.VMEM((1, 2 * tq), F32),
            pltpu.VMEM((ATT_VD + BF16_SUBLANES, 2 * tq), F32),
        ],
        compiler_params=pltpu.CompilerParams(
            dimension_semantics=("arbitrary", "arbitrary", "arbitrary"),
            vmem_limit_bytes=VMEM_LIMIT),
        name="diff_attention",
    )(lamv, proj, proj, proj, g_sub)


def _out_proj_kernel(ax_ref, ab_ref, ac_ref, axp_ref, acp_ref, axn_ref, acn_ref, cw_ref,
                     att_ref, oc_ref, x_ref, w_ref, o_ref, mix_sc, *, pos_blocks):
    i = pl.program_id(0)
    j = pl.program_id(1)
    tm, cc = ax_ref.shape

    @pl.when(j == 0)
    def _():
        z = ac_ref[...].astype(F32) * ax_ref[...].astype(F32)
        last = BF16_SUBLANES - 1
        zp = acp_ref[last:last + 1, :].astype(F32) * axp_ref[last:last + 1, :].astype(F32)
        zn = acn_ref[0:1, :].astype(F32) * axn_ref[0:1, :].astype(F32)
        zp = jnp.where(i % pos_blocks == 0, jnp.zeros_like(zp), zp)
        zn = jnp.where(i % pos_blocks == pos_blocks - 1, jnp.zeros_like(zn), zn)
        row = lax.broadcasted_iota(jnp.int32, z.shape, 0)
        z_prev = jnp.where(row == 0, zp, pltpu.roll(z, 1, 0))
        z_next = jnp.where(row == tm - 1, zn, pltpu.roll(z, tm - 1, 0))
        cw = cw_ref[...]
        conv = cw[0:1] * z_prev + cw[1:2] * z + cw[2:3] * z_next
        mix_sc[:, 0:cc] = (ab_ref[...].astype(F32) * conv).astype(BF16)
        aw = att_ref.shape[1]
        mix_sc[:, cc:cc + aw] = att_ref[...]
        mix_sc[:, cc + aw:] = oc_ref[...]

    o_ref[...] = x_ref[...] + jnp.dot(mix_sc[...], w_ref[...], preferred_element_type=F32)


def _out_proj(proj, att, x, conv_w, w_out, *, seq, tm, tn, oc_col):
    t, d = x.shape
    pos_blocks = seq // tm
    hb = tm // BF16_SUBLANES
    n_hb = t // BF16_SUBLANES
    prev = lambda i: jnp.maximum(i * hb - 1, 0)
    nxt = lambda i: jnp.minimum((i + 1) * hb, n_hb - 1)
    kern = functools.partial(_out_proj_kernel, pos_blocks=pos_blocks)
    return pl.pallas_call(
        kern,
        grid=(t // tm, d // tn),
        in_specs=[
            pl.BlockSpec((tm, COL), lambda i, j: (i, 0)),
            pl.BlockSpec((tm, COL), lambda i, j: (i, 1)),
            pl.BlockSpec((tm, COL), lambda i, j: (i, 2)),
            pl.BlockSpec((BF16_SUBLANES, COL), lambda i, j: (prev(i), 0)),
            pl.BlockSpec((BF16_SUBLANES, COL), lambda i, j: (prev(i), 2)),
            pl.BlockSpec((BF16_SUBLANES, COL), lambda i, j: (nxt(i), 0)),
            pl.BlockSpec((BF16_SUBLANES, COL), lambda i, j: (nxt(i), 2)),
            pl.BlockSpec(conv_w.shape, lambda i, j: (0, 0)),
            pl.BlockSpec((tm, att.shape[1]), lambda i, j: (i, 0)),
            pl.BlockSpec((tm, COL), lambda i, j: (i, oc_col)),
            pl.BlockSpec((tm, tn), lambda i, j: (i, j)),
            pl.BlockSpec((w_out.shape[0], tn), lambda i, j: (0, j)),
        ],
        out_specs=pl.BlockSpec((tm, tn), lambda i, j: (i, j)),
        out_shape=jax.ShapeDtypeStruct((t, d), F32),
        scratch_shapes=[pltpu.VMEM((tm, w_out.shape[0]), BF16)],
        compiler_params=pltpu.CompilerParams(
            dimension_semantics=("parallel", "arbitrary"), vmem_limit_bytes=VMEM_LIMIT),
        name="out_proj",
    )(proj, proj, proj, proj, proj, proj, proj, conv_w, att, proj, x, w_out)


def _ffn_kernel(x_ref, g_ref, wu_ref, wd_ref, o_ref, h_sc):
    f = pl.program_id(1)

    @pl.when(f == 0)
    def _():
        x = x_ref[...]
        h_sc[...] = _rms(x, g_ref[...]).astype(BF16)
        o_ref[...] = x

    hid = jnp.dot(h_sc[...], wu_ref[...], preferred_element_type=F32)
    hid = jnp.square(jnp.maximum(hid, 0.0)).astype(BF16)
    o_ref[...] += jnp.dot(hid, wd_ref[...], preferred_element_type=F32)


def _ffn(x, g2, w_up, w_down, *, tm, tf):
    t, d = x.shape
    ff = w_up.shape[1]
    return pl.pallas_call(
        _ffn_kernel,
        grid=(t // tm, ff // tf),
        in_specs=[
            pl.BlockSpec((tm, d), lambda i, f: (i, 0)),
            pl.BlockSpec((1, d), lambda i, f: (0, 0)),
            pl.BlockSpec((d, tf), lambda i, f: (0, f)),
            pl.BlockSpec((tf, d), lambda i, f: (f, 0)),
        ],
        out_specs=pl.BlockSpec((tm, d), lambda i, f: (i, 0)),
        out_shape=jax.ShapeDtypeStruct((t, d), F32),
        scratch_shapes=[pltpu.VMEM((tm, d), BF16)],
        compiler_params=pltpu.CompilerParams(
            dimension_semantics=("parallel", "arbitrary"), vmem_limit_bytes=VMEM_LIMIT),
        name="ffn",
    )(x, g2, w_up, w_down)


def _rope_tables(seq, rot_dim):
    half = rot_dim // 2
    inv = ROPE_THETA ** (-jnp.arange(0, rot_dim, 2, dtype=F32) / rot_dim)
    ang = jnp.arange(seq, dtype=F32)[:, None] * inv[None, :]
    cos, sin = jnp.cos(ang), jnp.sin(ang)
    pad = ATT_HD - rot_dim
    ones = jnp.ones((seq, pad), F32)
    zeros = jnp.zeros((seq, pad), F32)
    zh = jnp.zeros((seq, half), F32)
    rep = LANES // ATT_HD
    cos_t = jnp.tile(jnp.concatenate([cos, cos, ones], axis=1), (1, rep))
    sa_t = jnp.tile(jnp.concatenate([-sin, zh, zeros], axis=1), (1, rep))
    sb_t = jnp.tile(jnp.concatenate([zh, sin, zeros], axis=1), (1, rep))
    return cos_t, sa_t, sb_t


def _prep_layer(l, norm1_g, w_in, conv_w, q_norm_g, k_norm_g, lam_q1, lam_k1, lam_q2, lam_k2,
                subln_g, sgu_norm_g, sgu_w, sgu_b, w_out, norm2_g, w_up, w_down):
    d = w_in.shape[1]
    mlp_heads = sgu_w.shape[1]
    grp = jnp.arange(MXU_DIM) // ATT_HD
    seg = jnp.where(grp[:, None] == grp[None, :], 1.0 / ATT_HD, 0.0).astype(BF16)
    return dict(
        g1=norm1_g[l].reshape(1, d),
        w_in=w_in[l].astype(BF16),
        conv_w=conv_w[l],
        gq=jnp.tile(q_norm_g[l], COL // ATT_HD).reshape(1, COL),
        gk=jnp.tile(k_norm_g[l], COL // ATT_HD).reshape(1, COL),
        seg=seg,
        lamv=jnp.stack([lam_q1[l], lam_k1[l], lam_q2[l], lam_k2[l]]).astype(F32),
        g_sub=subln_g[l].reshape(1, ATT_VD),
        gv=jnp.tile(sgu_norm_g[l], mlp_heads).reshape(1, mlp_heads * MLP_HD),
        ws=sgu_w[l].astype(BF16),
        bs=jnp.repeat(sgu_b[l].T, MLP_HD, axis=1),
        w_out=w_out[l].astype(BF16),
        g2=norm2_g[l].reshape(1, d),
        w_up=w_up[l].astype(BF16),
        w_down=w_down[l].astype(BF16),
        lambda_init=0.8 - 0.6 * math.exp(-0.3 * l),
    )


def _trunk(x, layers, tm=1024, tq=512, tk=512, tn=512, tf=512):
    b, s, d = x.shape
    t = b * s
    heads = (d // 2) // ATT_VD
    tabs = _rope_tables(s, ATT_HD // 4)
    xf = x.reshape(t, d)
    for p in layers:
        proj = _in_proj(xf, p["g1"], p["w_in"], tabs, p["gq"], p["gk"], p["seg"], p["gv"],
                        p["ws"], p["bs"], seq=s, tm=tm)
        att = _attention(proj, p["lamv"], p["g_sub"], batch=b, seq=s, heads=heads,
                         q_col=12, k_col=12 + heads, v_col=12 + 2 * heads,
                         lambda_init=p["lambda_init"], tq=tq, tk=tk)
        x1 = _out_proj(proj, att, xf, p["conv_w"], p["w_out"], seq=s, tm=tm, tn=tn, oc_col=9)
        xf = _ffn(x1, p["g2"], p["w_up"], p["w_down"], tm=tm, tf=tf)
    return xf.reshape(b, s, d)


def kernel(x_prompt, x_sample, norm1_g, w_in, conv_w, q_norm_g, k_norm_g, lam_q1, lam_k1, lam_q2,
           lam_k2, subln_g, sgu_norm_g, sgu_w, sgu_b, w_out, norm2_g, w_up, w_down):
    params = (norm1_g, w_in, conv_w, q_norm_g, k_norm_g, lam_q1, lam_k1, lam_q2, lam_k2,
              subln_g, sgu_norm_g, sgu_w, sgu_b, w_out, norm2_g, w_up, w_down)
    layers = [_prep_layer(l, *params) for l in range(w_in.shape[0])]
    return (_trunk(x_prompt, layers), _trunk(x_sample, layers))
```

```python
import functools
import math

import jax
import jax.numpy as jnp
from jax import lax
from jax.experimental import pallas as pl
from jax.experimental.pallas import tpu as pltpu

F32 = jnp.float32
BF16 = jnp.bfloat16

EPS = 1e-6
ROPE_THETA = 500000.0
CONV_WIDTH = 3
ATT_HD = 64
ATT_VD = 2 * ATT_HD
KEY_CHUNK = 128
TRIPS_PER_LOOP = 32
MAX_LEAD = 64.0
CHUNK = 128
MLP_HD = 128
LANES = 128
MXU_DIM = 256
BF16_SUBLANES = 16
COL = 512
VMEM_LIMIT = 56 * 1024 * 1024
LOG2E = math.log2(math.e)


def _rms(x, gain):
    ms = jnp.mean(x * x, axis=-1, keepdims=True)
    return x * lax.rsqrt(ms + EPS) * gain


def _in_proj_kernel(x_ref, g1_ref, w_ref, cos_ref, sa_ref, sb_ref, gq_ref, gk_ref, seg_ref,
                    gv_ref, ws_ref, bs_ref, o_ref, h_sc, u_sc, *, n_conv, n_qk, n_v):
    j = pl.program_id(1)
    tm = x_ref.shape[0]

    @pl.when(j == 0)
    def _():
        h_sc[...] = _rms(x_ref[...], g1_ref[...]).astype(BF16)

    acc = jnp.dot(h_sc[...], w_ref[...], preferred_element_type=F32)

    q_lo, k_lo = n_conv, n_conv + n_qk
    v_lo, u_at = k_lo + n_qk, k_lo + n_qk + n_v

    def qk_norm_rope(gain):
        sq = (acc * acc).astype(BF16)
        ms = jnp.concatenate(
            [jnp.dot(sq[:, c:c + MXU_DIM], seg_ref[...], preferred_element_type=F32)
             for c in range(0, COL, MXU_DIM)], axis=1)
        y = acc * lax.rsqrt(ms + EPS) * gain
        cos, sa, sb = cos_ref[...], sa_ref[...], sb_ref[...]
        for c in range(COL // LANES):
            yc = y[:, c * LANES:(c + 1) * LANES]
            r = yc * cos + pltpu.roll(yc, LANES - 8, 1) * sa + pltpu.roll(yc, 8, 1) * sb
            o_ref[:, c * LANES:(c + 1) * LANES] = r.astype(BF16)

    @pl.when((j < q_lo) | ((j >= v_lo) & (j < u_at)))
    def _():
        o_ref[...] = acc.astype(BF16)

    @pl.when((j >= q_lo) & (j < k_lo))
    def _():
        qk_norm_rope(gq_ref[...] * (ATT_HD ** -0.5 * LOG2E))

    @pl.when((j >= k_lo) & (j < v_lo))
    def _():
        qk_norm_rope(gk_ref[...])

    @pl.when(j == u_at)
    def _():
        u_sc[...] = jax.nn.gelu(acc)

    @pl.when(j == u_at + 1)
    def _():
        v = jax.nn.gelu(acc)
        for h in range(COL // MLP_HD):
            sl = slice(h * MLP_HD, (h + 1) * MLP_HD)
            vn = _rms(v[:, sl], gv_ref[:, sl]).astype(BF16)
            w = ws_ref[h]
            for n in range(tm // CHUNK):
                rows = slice(n * CHUNK, (n + 1) * CHUNK)
                mixed = jnp.dot(w, vn[rows, :], preferred_element_type=F32) + bs_ref[:, sl]
                o_ref[rows, sl] = (u_sc[rows, sl] * mixed).astype(BF16)


def _in_proj(x, g1, w_in, tabs, gq, gk, seg, gv, ws, bs, *, seq, tm):
    t, d = x.shape
    n_in = w_in.shape[1] // COL
    n_out = n_in - 1
    pos_blocks = seq // tm
    const = lambda i, j: (0, 0)
    kern = functools.partial(_in_proj_kernel, n_conv=3, n_qk=2, n_v=2)
    return pl.pallas_call(
        kern,
        grid=(t // tm, n_in),
        in_specs=[
            pl.BlockSpec((tm, d), lambda i, j: (i, 0)),
            pl.BlockSpec((1, d), const),
            pl.BlockSpec((d, COL), lambda i, j: (0, j)),
            pl.BlockSpec((tm, LANES), lambda i, j: (i % pos_blocks, 0)),
            pl.BlockSpec((tm, LANES), lambda i, j: (i % pos_blocks, 0)),
            pl.BlockSpec((tm, LANES), lambda i, j: (i % pos_blocks, 0)),
            pl.BlockSpec((1, COL), const),
            pl.BlockSpec((1, COL), const),
            pl.BlockSpec((MXU_DIM, MXU_DIM), const),
            pl.BlockSpec((1, COL), const),
            pl.BlockSpec(ws.shape, lambda i, j: (0, 0, 0)),
            pl.BlockSpec((CHUNK, COL), const),
        ],
        out_specs=pl.BlockSpec((tm, COL), lambda i, j: (i, jnp.minimum(j, n_out - 1))),
        out_shape=jax.ShapeDtypeStruct((t, n_out * COL), BF16),
        scratch_shapes=[pltpu.VMEM((tm, d), BF16), pltpu.VMEM((tm, COL), F32)],
        compiler_params=pltpu.CompilerParams(
            dimension_semantics=("parallel", "arbitrary"), vmem_limit_bytes=VMEM_LIMIT),
        name="in_proj",
    )(x, g1, w_in, *tabs, gq, gk, seg, gv, ws, bs)


def _attn_kernel(lamv_ref, q_ref, k_ref, v_ref, g_ref, o_ref,
                 qst_sc, vt_sc, s_sc, p_sc, al_sc, m_sc, lead_sc, acc_sc, *, lambda_init):
    i = pl.program_id(2)
    tq = q_ref.shape[0]
    nk, _, tk = vt_sc.shape
    r = 2 * tq

    @pl.when(i == 0)
    def _():
        ones = jnp.ones((BF16_SUBLANES, tk), BF16)
        for c in range(nk):
            vt_sc[c, 0:ATT_VD, :] = v_ref[c * tk:(c + 1) * tk, :].astype(F32).T.astype(BF16)
            vt_sc[c, ATT_VD:, :] = ones

    qt = q_ref[...].astype(F32).T
    row = lax.broadcasted_iota(jnp.int32, qt.shape, 0)
    zero = jnp.zeros_like(qt)
    qst_sc[:, 0:tq] = jnp.where(row < ATT_HD, qt, zero).astype(BF16)
    qst_sc[:, tq:r] = jnp.where(row >= ATT_HD, qt, zero).astype(BF16)
    n_chunks = tk // KEY_CHUNK

    def score_chunk(j, c):
        off = pl.multiple_of(j * tk + c * KEY_CHUNK, KEY_CHUNK)
        return jnp.dot(k_ref[pl.ds(off, KEY_CHUNK), :], qst_sc[...],
                       preferred_element_type=F32)

    def add_values(j, slot):
        acc_sc[...] = al_sc[slot] * (acc_sc[...] + jnp.dot(
            vt_sc[j], p_sc[slot], preferred_element_type=F32))

    m_sc[...] = jnp.max(score_chunk(0, 0), axis=0, keepdims=True)
    lead_sc[...] = jnp.zeros(lead_sc.shape, F32)
    acc_sc[...] = jnp.zeros(acc_sc.shape, F32)
    p_sc[1] = jnp.zeros(p_sc.shape[1:], BF16)
    al_sc[1] = jnp.ones(al_sc.shape[1:], F32)

    def trip(j, slot):
        m_prev = m_sc[...]
        bm = None
        for c in range(n_chunks):
            rows = slice(c * KEY_CHUNK, (c + 1) * KEY_CHUNK)
            s = score_chunk(j, c)
            p_sc[slot, rows, :] = jnp.exp2(s - m_prev).astype(BF16)
            cmx = jnp.max(s, axis=0, keepdims=True)
            bm = cmx if bm is None else jnp.maximum(bm, cmx)
            if c == max(n_chunks // 2 - 1, 0):
                add_values(max(j - 1, 0) if isinstance(j, int) else jnp.maximum(j - 1, 0),
                           1 - slot)
        m_new = jnp.maximum(m_prev, bm)
        m_sc[...] = m_new
        al_sc[slot] = jnp.exp2(m_prev - m_new)
        lead_sc[...] = jnp.maximum(lead_sc[...], bm - m_prev)

    n_loops = nk // TRIPS_PER_LOOP

    def unrolled(jj, carry):
        j0 = TRIPS_PER_LOOP * jj
        for u in range(TRIPS_PER_LOOP):
            trip(j0 + u, u % 2)
        return carry

    lax.fori_loop(0, n_loops, unrolled, 0)
    for j in range(n_loops * TRIPS_PER_LOOP, nk):
        trip(j, j % 2)
    add_values(nk - 1, (nk - 1) % 2)

    @pl.when(jnp.max(lead_sc[...]) > MAX_LEAD)
    def _():
        m_sc[...] = jnp.full(m_sc.shape, -jnp.inf, F32)
        acc_sc[...] = jnp.zeros(acc_sc.shape, F32)

        def block(j, carry):
            m_prev = m_sc[...]
            for c in range(n_chunks):
                s_sc[c * KEY_CHUNK:(c + 1) * KEY_CHUNK, :] = score_chunk(j, c)
            s = s_sc[...]
            m_new = jnp.maximum(m_prev, jnp.max(s, axis=0, keepdims=True))
            m_sc[...] = m_new
            acc_sc[...] = jnp.exp2(m_prev - m_new) * acc_sc[...] + jnp.dot(
                vt_sc[j], jnp.exp2(s - m_new).astype(BF16), preferred_element_type=F32)
            return carry

        lax.fori_loop(0, nk, block, 0)

    lv = lamv_ref[...]
    lam = (jnp.exp(jnp.sum(lv[0:1] * lv[1:2], axis=1, keepdims=True))
           - jnp.exp(jnp.sum(lv[2:3] * lv[3:4], axis=1, keepdims=True)) + lambda_init)
    o_all = acc_sc[0:ATT_VD, :] / acc_sc[ATT_VD:ATT_VD + 1, :]
    ot = o_all[:, 0:tq] - lam * o_all[:, tq:r]
    ms = jnp.mean(ot * ot, axis=0, keepdims=True)
    o = (ot * lax.rsqrt(ms + EPS)).T * g_ref[...]
    o_ref[...] = (o * (1.0 - lambda_init)).astype(BF16)


def _attention(proj, lamv, g_sub, *, batch, seq, heads, q_col, k_col, v_col, lambda_init, tq, tk):
    nq = seq // tq
    nk = seq // tk
    assert nk >= 2 and TRIPS_PER_LOOP % 2 == 0, "slots alternate with block parity"
    kern = functools.partial(_attn_kernel, lambda_init=lambda_init)
    return pl.pallas_call(
        kern,
        grid=(batch, heads, nq),
        in_specs=[
            pl.BlockSpec(lamv.shape, lambda b, h, i: (0, 0)),
            pl.BlockSpec((tq, ATT_VD), lambda b, h, i: (b * nq + i, q_col + h)),
            pl.BlockSpec((seq, ATT_VD), lambda b, h, i: (b, k_col + h)),
            pl.BlockSpec((seq, ATT_VD), lambda b, h, i: (b, v_col + h)),
            pl.BlockSpec((1, ATT_VD), lambda b, h, i: (0, 0)),
        ],
        out_specs=pl.BlockSpec((tq, ATT_VD), lambda b, h, i: (b * nq + i, h)),
        out_shape=jax.ShapeDtypeStruct((batch * seq, heads * ATT_VD), BF16),
        scratch_shapes=[
            pltpu.VMEM((ATT_VD, 2 * tq), BF16),
            pltpu.VMEM((nk, ATT_VD + BF16_SUBLANES, tk), BF16),
            pltpu.VMEM((tk, 2 * tq), F32),
            pltpu.VMEM((2, tk, 2 * tq), BF16),
            pltpu.VMEM((2, 1, 2 * tq), F32),
            pltpu.VMEM((1, 2 * tq), F32),
            pltpu.VMEM((1, 2 * tq), F32),
            pltpu.VMEM((ATT_VD + BF16_SUBLANES, 2 * tq), F32),
        ],
        compiler_params=pltpu.CompilerParams(
            dimension_semantics=("arbitrary", "arbitrary", "arbitrary"),
            vmem_limit_bytes=VMEM_LIMIT),
        name="diff_attention",
    )(lamv, proj, proj, proj, g_sub)


def _out_proj_kernel(ax_ref, ab_ref, ac_ref, axp_ref, acp_ref, axn_ref, acn_ref, cw_ref,
                     att_ref, oc_ref, x_ref, w_ref, o_ref, mix_sc, *, pos_blocks):
    i = pl.program_id(0)
    j = pl.program_id(1)
    tm, cc = ax_ref.shape

    @pl.when(j == 0)
    def _():
        z = ac_ref[...].astype(F32) * ax_ref[...].astype(F32)
        last = BF16_SUBLANES - 1
        zp = acp_ref[last:last + 1, :].astype(F32) * axp_ref[last:last + 1, :].astype(F32)
        zn = acn_ref[0:1, :].astype(F32) * axn_ref[0:1, :].astype(F32)
        zp = jnp.where(i % pos_blocks == 0, jnp.zeros_like(zp), zp)
        zn = jnp.where(i % pos_blocks == pos_blocks - 1, jnp.zeros_like(zn), zn)
        row = lax.broadcasted_iota(jnp.int32, z.shape, 0)
        z_prev = jnp.where(row == 0, zp, pltpu.roll(z, 1, 0))
        z_next = jnp.where(row == tm - 1, zn, pltpu.roll(z, tm - 1, 0))
        cw = cw_ref[...]
        conv = cw[0:1] * z_prev + cw[1:2] * z + cw[2:3] * z_next
        mix_sc[:, 0:cc] = (ab_ref[...].astype(F32) * conv).astype(BF16)
        aw = att_ref.shape[1]
        mix_sc[:, cc:cc + aw] = att_ref[...]
        mix_sc[:, cc + aw:] = oc_ref[...]

    o_ref[...] = x_ref[...] + jnp.dot(mix_sc[...], w_ref[...], preferred_element_type=F32)


def _out_proj(proj, att, x, conv_w, w_out, *, seq, tm, tn, oc_col):
    t, d = x.shape
    pos_blocks = seq // tm
    hb = tm // BF16_SUBLANES
    n_hb = t // BF16_SUBLANES
    prev = lambda i: jnp.maximum(i * hb - 1, 0)
    nxt = lambda i: jnp.minimum((i + 1) * hb, n_hb - 1)
    kern = functools.partial(_out_proj_kernel, pos_blocks=pos_blocks)
    return pl.pallas_call(
        kern,
        grid=(t // tm, d // tn),
        in_specs=[
            pl.BlockSpec((tm, COL), lambda i, j: (i, 0)),
            pl.BlockSpec((tm, COL), lambda i, j: (i, 1)),
            pl.BlockSpec((tm, COL), lambda i, j: (i, 2)),
            pl.BlockSpec((BF16_SUBLANES, COL), lambda i, j: (prev(i), 0)),
            pl.BlockSpec((BF16_SUBLANES, COL), lambda i, j: (prev(i), 2)),
            pl.BlockSpec((BF16_SUBLANES, COL), lambda i, j: (nxt(i), 0)),
            pl.BlockSpec((BF16_SUBLANES, COL), lambda i, j: (nxt(i), 2)),
            pl.BlockSpec(conv_w.shape, lambda i, j: (0, 0)),
            pl.BlockSpec((tm, att.shape[1]), lambda i, j: (i, 0)),
            pl.BlockSpec((tm, COL), lambda i, j: (i, oc_col)),
            pl.BlockSpec((tm, tn), lambda i, j: (i, j)),
            pl.BlockSpec((w_out.shape[0], tn), lambda i, j: (0, j)),
        ],
        out_specs=pl.BlockSpec((tm, tn), lambda i, j: (i, j)),
        out_shape=jax.ShapeDtypeStruct((t, d), F32),
        scratch_shapes=[pltpu.VMEM((tm, w_out.shape[0]), BF16)],
        compiler_params=pltpu.CompilerParams(
            dimension_semantics=("parallel", "arbitrary"), vmem_limit_bytes=VMEM_LIMIT),
        name="out_proj",
    )(proj, proj, proj, proj, proj, proj, proj, conv_w, att, proj, x, w_out)


def _ffn_kernel(x_ref, g_ref, wu_ref, wd_ref, o_ref, h_sc):
    f = pl.program_id(1)

    @pl.when(f == 0)
    def _():
        x = x_ref[...]
        h_sc[...] = _rms(x, g_ref[...]).astype(BF16)
        o_ref[...] = x

    hid = jnp.dot(h_sc[...], wu_ref[...], preferred_element_type=F32)
    hid = jnp.square(jnp.maximum(hid, 0.0)).astype(BF16)
    o_ref[...] += jnp.dot(hid, wd_ref[...], preferred_element_type=F32)


def _ffn(x, g2, w_up, w_down, *, tm, tf):
    t, d = x.shape
    ff = w_up.shape[1]
    return pl.pallas_call(
        _ffn_kernel,
        grid=(t // tm, ff // tf),
        in_specs=[
            pl.BlockSpec((tm, d), lambda i, f: (i, 0)),
            pl.BlockSpec((1, d), lambda i, f: (0, 0)),
            pl.BlockSpec((d, tf), lambda i, f: (0, f)),
            pl.BlockSpec((tf, d), lambda i, f: (f, 0)),
        ],
        out_specs=pl.BlockSpec((tm, d), lambda i, f: (i, 0)),
        out_shape=jax.ShapeDtypeStruct((t, d), F32),
        scratch_shapes=[pltpu.VMEM((tm, d), BF16)],
        compiler_params=pltpu.CompilerParams(
            dimension_semantics=("parallel", "arbitrary"), vmem_limit_bytes=VMEM_LIMIT),
        name="ffn",
    )(x, g2, w_up, w_down)


def _rope_tables(seq, rot_dim):
    half = rot_dim // 2
    inv = ROPE_THETA ** (-jnp.arange(0, rot_dim, 2, dtype=F32) / rot_dim)
    ang = jnp.arange(seq, dtype=F32)[:, None] * inv[None, :]
    cos, sin = jnp.cos(ang), jnp.sin(ang)
    pad = ATT_HD - rot_dim
    ones = jnp.ones((seq, pad), F32)
    zeros = jnp.zeros((seq, pad), F32)
    zh = jnp.zeros((seq, half), F32)
    rep = LANES // ATT_HD
    cos_t = jnp.tile(jnp.concatenate([cos, cos, ones], axis=1), (1, rep))
    sa_t = jnp.tile(jnp.concatenate([-sin, zh, zeros], axis=1), (1, rep))
    sb_t = jnp.tile(jnp.concatenate([zh, sin, zeros], axis=1), (1, rep))
    return cos_t, sa_t, sb_t


def _prep_layer(l, norm1_g, w_in, conv_w, q_norm_g, k_norm_g, lam_q1, lam_k1, lam_q2, lam_k2,
                subln_g, sgu_norm_g, sgu_w, sgu_b, w_out, norm2_g, w_up, w_down):
    d = w_in.shape[1]
    mlp_heads = sgu_w.shape[1]
    grp = jnp.arange(MXU_DIM) // ATT_HD
    seg = jnp.where(grp[:, None] == grp[None, :], 1.0 / ATT_HD, 0.0).astype(BF16)
    return dict(
        g1=norm1_g[l].reshape(1, d),
        w_in=w_in[l].astype(BF16),
        conv_w=conv_w[l],
        gq=jnp.tile(q_norm_g[l], COL // ATT_HD).reshape(1, COL),
        gk=jnp.tile(k_norm_g[l], COL // ATT_HD).reshape(1, COL),
        seg=seg,
        lamv=jnp.stack([lam_q1[l], lam_k1[l], lam_q2[l], lam_k2[l]]).astype(F32),
        g_sub=subln_g[l].reshape(1, ATT_VD),
        gv=jnp.tile(sgu_norm_g[l], mlp_heads).reshape(1, mlp_heads * MLP_HD),
        ws=sgu_w[l].astype(BF16),
        bs=jnp.repeat(sgu_b[l].T, MLP_HD, axis=1),
        w_out=w_out[l].astype(BF16),
        g2=norm2_g[l].reshape(1, d),
        w_up=w_up[l].astype(BF16),
        w_down=w_down[l].astype(BF16),
        lambda_init=0.8 - 0.6 * math.exp(-0.3 * l),
    )


def _trunk(x, layers, tm=1024, tq=512, tk=512, tn=512, tf=512):
    b, s, d = x.shape
    t = b * s
    heads = (d // 2) // ATT_VD
    tabs = _rope_tables(s, ATT_HD // 4)
    xf = x.reshape(t, d)
    for p in layers:
        proj = _in_proj(xf, p["g1"], p["w_in"], tabs, p["gq"], p["gk"], p["seg"], p["gv"],
                        p["ws"], p["bs"], seq=s, tm=tm)
        att = _attention(proj, p["lamv"], p["g_sub"], batch=b, seq=s, heads=heads,
                         q_col=12, k_col=12 + heads, v_col=12 + 2 * heads,
                         lambda_init=p["lambda_init"], tq=tq, tk=tk)
        x1 = _out_proj(proj, att, xf, p["conv_w"], p["w_out"], seq=s, tm=tm, tn=tn, oc_col=9)
        xf = _ffn(x1, p["g2"], p["w_up"], p["w_down"], tm=tm, tf=tf)
    return xf.reshape(b, s, d)


def kernel(x_prompt, x_sample, norm1_g, w_in, conv_w, q_norm_g, k_norm_g, lam_q1, lam_k1, lam_q2,
           lam_k2, subln_g, sgu_norm_g, sgu_w, sgu_b, w_out, norm2_g, w_up, w_down):
    params = (norm1_g, w_in, conv_w, q_norm_g, k_norm_g, lam_q1, lam_k1, lam_q2, lam_k2,
              subln_g, sgu_norm_g, sgu_w, sgu_b, w_out, norm2_g, w_up, w_down)
    layers = [_prep_layer(l, *params) for l in range(w_in.shape[0])]
    return (_trunk(x_prompt, layers), _trunk(x_sample, layers))
```

```python
import functools
import math

import jax
import jax.numpy as jnp
from jax import lax
from jax.experimental import pallas as pl
from jax.experimental.pallas import tpu as pltpu

F32 = jnp.float32
BF16 = jnp.bfloat16

EPS = 1e-6
ROPE_THETA = 500000.0
CONV_WIDTH = 3
ATT_HD = 64
ATT_VD = 2 * ATT_HD
KEY_CHUNK = 128
TRIPS_PER_LOOP = 32
MAX_LEAD = 64.0
CHUNK = 128
MLP_HD = 128
LANES = 128
MXU_DIM = 256
BF16_SUBLANES = 16
COL = 512
VMEM_LIMIT = 56 * 1024 * 1024
LOG2E = math.log2(math.e)


def _rms(x, gain):
    ms = jnp.mean(x * x, axis=-1, keepdims=True)
    return x * lax.rsqrt(ms + EPS) * gain


def _in_proj_kernel(x_ref, g1_ref, w_ref, cos_ref, sin_ref, gq_ref, gk_ref, seg_ref, perm_ref,
                    gv_ref, ws_ref, bs_ref, o_ref, h_sc, u_sc, *, n_conv, n_qk, n_v):
    j = pl.program_id(1)
    tm = x_ref.shape[0]

    @pl.when(j == 0)
    def _():
        h_sc[...] = _rms(x_ref[...], g1_ref[...]).astype(BF16)

    acc = jnp.dot(h_sc[...], w_ref[...], preferred_element_type=F32)

    q_lo, k_lo = n_conv, n_conv + n_qk
    v_lo, u_at = k_lo + n_qk, k_lo + n_qk + n_v

    def qk_norm_rope(gain):
        sq = (acc * acc).astype(BF16)
        ms = jnp.concatenate(
            [jnp.dot(sq[:, c:c + MXU_DIM], seg_ref[...], preferred_element_type=F32)
             for c in range(0, COL, MXU_DIM)], axis=1)
        y = acc * lax.rsqrt(ms + EPS) * gain
        yb = y.astype(BF16)
        cos, sin = cos_ref[...], sin_ref[...]
        for c in range(0, COL, MXU_DIM):
            rot = jnp.dot(yb[:, c:c + MXU_DIM], perm_ref[...], preferred_element_type=F32)
            for cc in range(c, c + MXU_DIM, LANES):
                r = y[:, cc:cc + LANES] * cos + rot[:, cc - c:cc - c + LANES] * sin
                o_ref[:, cc:cc + LANES] = r.astype(BF16)

    @pl.when((j < q_lo) | ((j >= v_lo) & (j < u_at)))
    def _():
        o_ref[...] = acc.astype(BF16)

    @pl.when((j >= q_lo) & (j < k_lo))
    def _():
        qk_norm_rope(gq_ref[...] * (ATT_HD ** -0.5 * LOG2E))

    @pl.when((j >= k_lo) & (j < v_lo))
    def _():
        qk_norm_rope(gk_ref[...])

    @pl.when(j == u_at)
    def _():
        u_sc[...] = jax.nn.gelu(acc)

    @pl.when(j == u_at + 1)
    def _():
        v = jax.nn.gelu(acc)
        for h in range(COL // MLP_HD):
            sl = slice(h * MLP_HD, (h + 1) * MLP_HD)
            vn = _rms(v[:, sl], gv_ref[:, sl]).astype(BF16)
            w = ws_ref[h]
            for n in range(tm // CHUNK):
                rows = slice(n * CHUNK, (n + 1) * CHUNK)
                mixed = jnp.dot(w, vn[rows, :], preferred_element_type=F32) + bs_ref[:, sl]
                o_ref[rows, sl] = (u_sc[rows, sl] * mixed).astype(BF16)


def _in_proj(x, g1, w_in, tabs, gq, gk, seg, perm, gv, ws, bs, *, seq, tm):
    t, d = x.shape
    n_in = w_in.shape[1] // COL
    n_out = n_in - 1
    pos_blocks = seq // tm
    const = lambda i, j: (0, 0)
    kern = functools.partial(_in_proj_kernel, n_conv=3, n_qk=2, n_v=2)
    return pl.pallas_call(
        kern,
        grid=(t // tm, n_in),
        in_specs=[
            pl.BlockSpec((tm, d), lambda i, j: (i, 0)),
            pl.BlockSpec((1, d), const),
            pl.BlockSpec((d, COL), lambda i, j: (0, j)),
            pl.BlockSpec((tm, LANES), lambda i, j: (i % pos_blocks, 0)),
            pl.BlockSpec((tm, LANES), lambda i, j: (i % pos_blocks, 0)),
            pl.BlockSpec((1, COL), const),
            pl.BlockSpec((1, COL), const),
            pl.BlockSpec((MXU_DIM, MXU_DIM), const),
            pl.BlockSpec((MXU_DIM, MXU_DIM), const),
            pl.BlockSpec((1, COL), const),
            pl.BlockSpec(ws.shape, lambda i, j: (0, 0, 0)),
            pl.BlockSpec((CHUNK, COL), const),
        ],
        out_specs=pl.BlockSpec((tm, COL), lambda i, j: (i, jnp.minimum(j, n_out - 1))),
        out_shape=jax.ShapeDtypeStruct((t, n_out * COL), BF16),
        scratch_shapes=[pltpu.VMEM((tm, d), BF16), pltpu.VMEM((tm, COL), F32)],
        compiler_params=pltpu.CompilerParams(
            dimension_semantics=("parallel", "arbitrary"), vmem_limit_bytes=VMEM_LIMIT),
        name="in_proj",
    )(x, g1, w_in, *tabs, gq, gk, seg, perm, gv, ws, bs)


def _attn_kernel(lamv_ref, q_ref, k_ref, v_ref, g_ref, o_ref,
                 qst_sc, vt_sc, s_sc, p_sc, al_sc, m_sc, lead_sc, acc_sc, *, lambda_init):
    i = pl.program_id(2)
    tq = q_ref.shape[0]
    nk, _, tk = vt_sc.shape
    r = 2 * tq

    @pl.when(i == 0)
    def _():
        ones = jnp.ones((BF16_SUBLANES, tk), BF16)
        for c in range(nk):
            vt_sc[c, 0:ATT_VD, :] = v_ref[c * tk:(c + 1) * tk, :].astype(F32).T.astype(BF16)
            vt_sc[c, ATT_VD:, :] = ones

    qt = q_ref[...].astype(F32).T
    row = lax.broadcasted_iota(jnp.int32, qt.shape, 0)
    zero = jnp.zeros_like(qt)
    qst_sc[:, 0:tq] = jnp.where(row < ATT_HD, qt, zero).astype(BF16)
    qst_sc[:, tq:r] = jnp.where(row >= ATT_HD, qt, zero).astype(BF16)
    n_chunks = tk // KEY_CHUNK

    def score_chunk(j, c):
        off = pl.multiple_of(j * tk + c * KEY_CHUNK, KEY_CHUNK)
        return jnp.dot(k_ref[pl.ds(off, KEY_CHUNK), :], qst_sc[...],
                       preferred_element_type=F32)

    def add_values(j, slot):
        acc_sc[...] = al_sc[slot] * (acc_sc[...] + jnp.dot(
            vt_sc[j], p_sc[slot], preferred_element_type=F32))

    m_sc[...] = jnp.max(score_chunk(0, 0), axis=0, keepdims=True)
    lead_sc[...] = jnp.zeros(lead_sc.shape, F32)
    acc_sc[...] = jnp.zeros(acc_sc.shape, F32)
    p_sc[1] = jnp.zeros(p_sc.shape[1:], BF16)
    al_sc[1] = jnp.ones(al_sc.shape[1:], F32)

    def trip(j, slot):
        m_prev = m_sc[...]
        bm = None
        for c in range(n_chunks):
            rows = slice(c * KEY_CHUNK, (c + 1) * KEY_CHUNK)
            s = score_chunk(j, c)
            p_sc[slot, rows, :] = jnp.exp2(s - m_prev).astype(BF16)
            cmx = jnp.max(s, axis=0, keepdims=True)
            bm = cmx if bm is None else jnp.maximum(bm, cmx)
            if c == max(n_chunks // 2 - 1, 0):
                add_values(max(j - 1, 0) if isinstance(j, int) else jnp.maximum(j - 1, 0),
                           1 - slot)
        m_new = jnp.maximum(m_prev, bm)
        m_sc[...] = m_new
        al_sc[slot] = jnp.exp2(m_prev - m_new)
        lead_sc[...] = jnp.maximum(lead_sc[...], bm - m_prev)

    n_loops = nk // TRIPS_PER_LOOP

    def unrolled(jj, carry):
        j0 = TRIPS_PER_LOOP * jj
        for u in range(TRIPS_PER_LOOP):
            trip(j0 + u, u % 2)
        return carry

    lax.fori_loop(0, n_loops, unrolled, 0)
    for j in range(n_loops * TRIPS_PER_LOOP, nk):
        trip(j, j % 2)
    add_values(nk - 1, (nk - 1) % 2)

    @pl.when(jnp.max(lead_sc[...]) > MAX_LEAD)
    def _():
        m_sc[...] = jnp.full(m_sc.shape, -jnp.inf, F32)
        acc_sc[...] = jnp.zeros(acc_sc.shape, F32)

        def block(j, carry):
            m_prev = m_sc[...]
            for c in range(n_chunks):
                s_sc[c * KEY_CHUNK:(c + 1) * KEY_CHUNK, :] = score_chunk(j, c)
            s = s_sc[...]
            m_new = jnp.maximum(m_prev, jnp.max(s, axis=0, keepdims=True))
            m_sc[...] = m_new
            acc_sc[...] = jnp.exp2(m_prev - m_new) * acc_sc[...] + jnp.dot(
                vt_sc[j], jnp.exp2(s - m_new).astype(BF16), preferred_element_type=F32)
            return carry

        lax.fori_loop(0, nk, block, 0)

    lv = lamv_ref[...]
    lam = (jnp.exp(jnp.sum(lv[0:1] * lv[1:2], axis=1, keepdims=True))
           - jnp.exp(jnp.sum(lv[2:3] * lv[3:4], axis=1, keepdims=True)) + lambda_init)
    o_all = acc_sc[0:ATT_VD, :] / acc_sc[ATT_VD:ATT_VD + 1, :]
    ot = o_all[:, 0:tq] - lam * o_all[:, tq:r]
    ms = jnp.mean(ot * ot, axis=0, keepdims=True)
    o = (ot * lax.rsqrt(ms + EPS)).T * g_ref[...]
    o_ref[...] = (o * (1.0 - lambda_init)).astype(BF16)


def _attention(proj, lamv, g_sub, *, batch, seq, heads, q_col, k_col, v_col, lambda_init, tq, tk):
    nq = seq // tq
    nk = seq // tk
    assert nk >= 2 and TRIPS_PER_LOOP % 2 == 0, "slots alternate with block parity"
    kern = functools.partial(_attn_kernel, lambda_init=lambda_init)
    return pl.pallas_call(
        kern,
        grid=(batch, heads, nq),
        in_specs=[
            pl.BlockSpec(lamv.shape, lambda b, h, i: (0, 0)),
            pl.BlockSpec((tq, ATT_VD), lambda b, h, i: (b * nq + i, q_col + h)),
            pl.BlockSpec((seq, ATT_VD), lambda b, h, i: (b, k_col + h)),
            pl.BlockSpec((seq, ATT_VD), lambda b, h, i: (b, v_col + h)),
            pl.BlockSpec((1, ATT_VD), lambda b, h, i: (0, 0)),
        ],
        out_specs=pl.BlockSpec((tq, ATT_VD), lambda b, h, i: (b * nq + i, h)),
        out_shape=jax.ShapeDtypeStruct((batch * seq, heads * ATT_VD), BF16),
        scratch_shapes=[
            pltpu.VMEM((ATT_VD, 2 * tq), BF16),
            pltpu.VMEM((nk, ATT_VD + BF16_SUBLANES, tk), BF16),
            pltpu.VMEM((tk, 2 * tq), F32),
            pltpu.VMEM((2, tk, 2 * tq), BF16),
            pltpu.VMEM((2, 1, 2 * tq), F32),
            pltpu.VMEM((1, 2 * tq), F32),
            pltpu.VMEM((1, 2 * tq), F32),
            pltpu.VMEM((ATT_VD + BF16_SUBLANES, 2 * tq), F32),
        ],
        compiler_params=pltpu.CompilerParams(
            dimension_semantics=("arbitrary", "arbitrary", "arbitrary"),
            vmem_limit_bytes=VMEM_LIMIT),
        name="diff_attention",
    )(lamv, proj, proj, proj, g_sub)


def _out_proj_kernel(ax_ref, ab_ref, ac_ref, axp_ref, acp_ref, axn_ref, acn_ref, cw_ref,
                     att_ref, oc_ref, x_ref, w_ref, o_ref, mix_sc, *, pos_blocks):
    i = pl.program_id(0)
    j = pl.program_id(1)
    tm, cc = ax_ref.shape

    @pl.when(j == 0)
    def _():
        z = ac_ref[...].astype(F32) * ax_ref[...].astype(F32)
        last = BF16_SUBLANES - 1
        zp = acp_ref[last:last + 1, :].astype(F32) * axp_ref[last:last + 1, :].astype(F32)
        zn = acn_ref[0:1, :].astype(F32) * axn_ref[0:1, :].astype(F32)
        zp = jnp.where(i % pos_blocks == 0, jnp.zeros_like(zp), zp)
        zn = jnp.where(i % pos_blocks == pos_blocks - 1, jnp.zeros_like(zn), zn)
        row = lax.broadcasted_iota(jnp.int32, z.shape, 0)
        z_prev = jnp.where(row == 0, zp, pltpu.roll(z, 1, 0))
        z_next = jnp.where(row == tm - 1, zn, pltpu.roll(z, tm - 1, 0))
        cw = cw_ref[...]
        conv = cw[0:1] * z_prev + cw[1:2] * z + cw[2:3] * z_next
        mix_sc[:, 0:cc] = (ab_ref[...].astype(F32) * conv).astype(BF16)
        aw = att_ref.shape[1]
        mix_sc[:, cc:cc + aw] = att_ref[...]
        mix_sc[:, cc + aw:] = oc_ref[...]

    o_ref[...] = x_ref[...] + jnp.dot(mix_sc[...], w_ref[...], preferred_element_type=F32)


def _out_proj(proj, att, x, conv_w, w_out, *, seq, tm, tn, oc_col):
    t, d = x.shape
    pos_blocks = seq // tm
    hb = tm // BF16_SUBLANES
    n_hb = t // BF16_SUBLANES
    prev = lambda i: jnp.maximum(i * hb - 1, 0)
    nxt = lambda i: jnp.minimum((i + 1) * hb, n_hb - 1)
    kern = functools.partial(_out_proj_kernel, pos_blocks=pos_blocks)
    return pl.pallas_call(
        kern,
        grid=(t // tm, d // tn),
        in_specs=[
            pl.BlockSpec((tm, COL), lambda i, j: (i, 0)),
            pl.BlockSpec((tm, COL), lambda i, j: (i, 1)),
            pl.BlockSpec((tm, COL), lambda i, j: (i, 2)),
            pl.BlockSpec((BF16_SUBLANES, COL), lambda i, j: (prev(i), 0)),
            pl.BlockSpec((BF16_SUBLANES, COL), lambda i, j: (prev(i), 2)),
            pl.BlockSpec((BF16_SUBLANES, COL), lambda i, j: (nxt(i), 0)),
            pl.BlockSpec((BF16_SUBLANES, COL), lambda i, j: (nxt(i), 2)),
            pl.BlockSpec(conv_w.shape, lambda i, j: (0, 0)),
            pl.BlockSpec((tm, att.shape[1]), lambda i, j: (i, 0)),
            pl.BlockSpec((tm, COL), lambda i, j: (i, oc_col)),
            pl.BlockSpec((tm, tn), lambda i, j: (i, j)),
            pl.BlockSpec((w_out.shape[0], tn), lambda i, j: (0, j)),
        ],
        out_specs=pl.BlockSpec((tm, tn), lambda i, j: (i, j)),
        out_shape=jax.ShapeDtypeStruct((t, d), F32),
        scratch_shapes=[pltpu.VMEM((tm, w_out.shape[0]), BF16)],
        compiler_params=pltpu.CompilerParams(
            dimension_semantics=("parallel", "arbitrary"), vmem_limit_bytes=VMEM_LIMIT),
        name="out_proj",
    )(proj, proj, proj, proj, proj, proj, proj, conv_w, att, proj, x, w_out)


def _ffn_kernel(x_ref, g_ref, wu_ref, wd_ref, o_ref, h_sc):
    f = pl.program_id(1)

    @pl.when(f == 0)
    def _():
        x = x_ref[...]
        h_sc[...] = _rms(x, g_ref[...]).astype(BF16)
        o_ref[...] = x

    hid = jnp.dot(h_sc[...], wu_ref[...], preferred_element_type=F32)
    hid = jnp.square(jnp.maximum(hid, 0.0)).astype(BF16)
    o_ref[...] += jnp.dot(hid, wd_ref[...], preferred_element_type=F32)


def _ffn(x, g2, w_up, w_down, *, tm, tf):
    t, d = x.shape
    ff = w_up.shape[1]
    return pl.pallas_call(
        _ffn_kernel,
        grid=(t // tm, ff // tf),
        in_specs=[
            pl.BlockSpec((tm, d), lambda i, f: (i, 0)),
            pl.BlockSpec((1, d), lambda i, f: (0, 0)),
            pl.BlockSpec((d, tf), lambda i, f: (0, f)),
            pl.BlockSpec((tf, d), lambda i, f: (f, 0)),
        ],
        out_specs=pl.BlockSpec((tm, d), lambda i, f: (i, 0)),
        out_shape=jax.ShapeDtypeStruct((t, d), F32),
        scratch_shapes=[pltpu.VMEM((tm, d), BF16)],
        compiler_params=pltpu.CompilerParams(
            dimension_semantics=("parallel", "arbitrary"), vmem_limit_bytes=VMEM_LIMIT),
        name="ffn",
    )(x, g2, w_up, w_down)


def _rope_tables(seq, rot_dim):
    half = rot_dim // 2
    inv = ROPE_THETA ** (-jnp.arange(0, rot_dim, 2, dtype=F32) / rot_dim)
    ang = jnp.arange(seq, dtype=F32)[:, None] * inv[None, :]
    cos, sin = jnp.cos(ang), jnp.sin(ang)
    pad = ATT_HD - rot_dim
    ones = jnp.ones((seq, pad), F32)
    zeros = jnp.zeros((seq, pad), F32)
    rep = LANES // ATT_HD
    cos_t = jnp.tile(jnp.concatenate([cos, cos, ones], axis=1), (1, rep))
    sin_t = jnp.tile(jnp.concatenate([-sin, sin, zeros], axis=1), (1, rep))
    return cos_t, sin_t


def _prep_layer(l, norm1_g, w_in, conv_w, q_norm_g, k_norm_g, lam_q1, lam_k1, lam_q2, lam_k2,
                subln_g, sgu_norm_g, sgu_w, sgu_b, w_out, norm2_g, w_up, w_down):
    d = w_in.shape[1]
    mlp_heads = sgu_w.shape[1]
    grp = jnp.arange(MXU_DIM) // ATT_HD
    seg = jnp.where(grp[:, None] == grp[None, :], 1.0 / ATT_HD, 0.0).astype(BF16)
    lane = jnp.arange(MXU_DIM) % ATT_HD
    half = ATT_HD // 8
    partner = jnp.arange(MXU_DIM) + jnp.where(lane < half, half, -half)
    perm = ((jnp.arange(MXU_DIM)[:, None] == partner[None, :])
            & (lane[None, :] < 2 * half)).astype(BF16)
    return dict(
        g1=norm1_g[l].reshape(1, d),
        w_in=w_in[l].astype(BF16),
        conv_w=conv_w[l],
        gq=jnp.tile(q_norm_g[l], COL // ATT_HD).reshape(1, COL),
        gk=jnp.tile(k_norm_g[l], COL // ATT_HD).reshape(1, COL),
        seg=seg,
        perm=perm,
        lamv=jnp.stack([lam_q1[l], lam_k1[l], lam_q2[l], lam_k2[l]]).astype(F32),
        g_sub=subln_g[l].reshape(1, ATT_VD),
        gv=jnp.tile(sgu_norm_g[l], mlp_heads).reshape(1, mlp_heads * MLP_HD),
        ws=sgu_w[l].astype(BF16),
        bs=jnp.repeat(sgu_b[l].T, MLP_HD, axis=1),
        w_out=w_out[l].astype(BF16),
        g2=norm2_g[l].reshape(1, d),
        w_up=w_up[l].astype(BF16),
        w_down=w_down[l].astype(BF16),
        lambda_init=0.8 - 0.6 * math.exp(-0.3 * l),
    )


def _trunk(x, layers, tm=1024, tq=512, tk=512, tn=512, tf=512):
    b, s, d = x.shape
    t = b * s
    heads = (d // 2) // ATT_VD
    tabs = _rope_tables(s, ATT_HD // 4)
    xf = x.reshape(t, d)
    for p in layers:
        proj = _in_proj(xf, p["g1"], p["w_in"], tabs, p["gq"], p["gk"], p["seg"], p["perm"],
                        p["gv"],
                        p["ws"], p["bs"], seq=s, tm=tm)
        att = _attention(proj, p["lamv"], p["g_sub"], batch=b, seq=s, heads=heads,
                         q_col=12, k_col=12 + heads, v_col=12 + 2 * heads,
                         lambda_init=p["lambda_init"], tq=tq, tk=tk)
        x1 = _out_proj(proj, att, xf, p["conv_w"], p["w_out"], seq=s, tm=tm, tn=tn, oc_col=9)
        xf = _ffn(x1, p["g2"], p["w_up"], p["w_down"], tm=tm, tf=tf)
    return xf.reshape(b, s, d)


def kernel(x_prompt, x_sample, norm1_g, w_in, conv_w, q_norm_g, k_norm_g, lam_q1, lam_k1, lam_q2,
           lam_k2, subln_g, sgu_norm_g, sgu_w, sgu_b, w_out, norm2_g, w_up, w_down):
    params = (norm1_g, w_in, conv_w, q_norm_g, k_norm_g, lam_q1, lam_k1, lam_q2, lam_k2,
              subln_g, sgu_norm_g, sgu_w, sgu_b, w_out, norm2_g, w_up, w_down)
    layers = [_prep_layer(l, *params) for l in range(w_in.shape[0])]
    return (_trunk(x_prompt, layers), _trunk(x_sample, layers))
```

```python
import functools
import math

import jax
import jax.numpy as jnp
from jax import lax
from jax.experimental import pallas as pl
from jax.experimental.pallas import tpu as pltpu

F32 = jnp.float32
BF16 = jnp.bfloat16

EPS = 1e-6
ROPE_THETA = 500000.0
CONV_WIDTH = 3
ATT_HD = 64
ATT_VD = 2 * ATT_HD
KEY_CHUNK = 128
TRIPS_PER_REGION = 8
MAX_LEAD = 64.0
CHUNK = 128
MLP_HD = 128
LANES = 128
MXU_DIM = 256
BF16_SUBLANES = 16
COL = 512
VMEM_LIMIT = 56 * 1024 * 1024
LOG2E = math.log2(math.e)


def _rms(x, gain):
    ms = jnp.mean(x * x, axis=-1, keepdims=True)
    return x * lax.rsqrt(ms + EPS) * gain


def _in_proj_kernel(x_ref, g1_ref, w_ref, cos_ref, sin_ref, gq_ref, gk_ref, seg_ref, perm_ref,
                    gv_ref, ws_ref, bs_ref, o_ref, h_sc, u_sc, *, n_conv, n_qk, n_v):
    j = pl.program_id(1)
    tm = x_ref.shape[0]

    @pl.when(j == 0)
    def _():
        h_sc[...] = _rms(x_ref[...], g1_ref[...]).astype(BF16)

    acc = jnp.dot(h_sc[...], w_ref[...], preferred_element_type=F32)

    q_lo, k_lo = n_conv, n_conv + n_qk
    v_lo, u_at = k_lo + n_qk, k_lo + n_qk + n_v

    def qk_norm_rope(gain):
        sq = (acc * acc).astype(BF16)
        ms = jnp.concatenate(
            [jnp.dot(sq[:, c:c + MXU_DIM], seg_ref[...], preferred_element_type=F32)
             for c in range(0, COL, MXU_DIM)], axis=1)
        y = acc * lax.rsqrt(ms + EPS) * gain
        yb = y.astype(BF16)
        cos, sin = cos_ref[...], sin_ref[...]
        for c in range(0, COL, MXU_DIM):
            rot = jnp.dot(yb[:, c:c + MXU_DIM], perm_ref[...], preferred_element_type=F32)
            for cc in range(c, c + MXU_DIM, LANES):
                r = y[:, cc:cc + LANES] * cos + rot[:, cc - c:cc - c + LANES] * sin
                o_ref[:, cc:cc + LANES] = r.astype(BF16)

    @pl.when((j < q_lo) | ((j >= v_lo) & (j < u_at)))
    def _():
        o_ref[...] = acc.astype(BF16)

    @pl.when((j >= q_lo) & (j < k_lo))
    def _():
        qk_norm_rope(gq_ref[...] * (ATT_HD ** -0.5 * LOG2E))

    @pl.when((j >= k_lo) & (j < v_lo))
    def _():
        qk_norm_rope(gk_ref[...])

    @pl.when(j == u_at)
    def _():
        u_sc[...] = jax.nn.gelu(acc)

    @pl.when(j == u_at + 1)
    def _():
        v = jax.nn.gelu(acc)
        for h in range(COL // MLP_HD):
            sl = slice(h * MLP_HD, (h + 1) * MLP_HD)
            vn = _rms(v[:, sl], gv_ref[:, sl]).astype(BF16)
            w = ws_ref[h]
            for n in range(tm // CHUNK):
                rows = slice(n * CHUNK, (n + 1) * CHUNK)
                mixed = jnp.dot(w, vn[rows, :], preferred_element_type=F32) + bs_ref[:, sl]
                o_ref[rows, sl] = (u_sc[rows, sl] * mixed).astype(BF16)


def _in_proj(x, g1, w_in, tabs, gq, gk, seg, perm, gv, ws, bs, *, seq, tm):
    t, d = x.shape
    n_in = w_in.shape[1] // COL
    n_out = n_in - 1
    pos_blocks = seq // tm
    const = lambda i, j: (0, 0)
    kern = functools.partial(_in_proj_kernel, n_conv=3, n_qk=2, n_v=2)
    return pl.pallas_call(
        kern,
        grid=(t // tm, n_in),
        in_specs=[
            pl.BlockSpec((tm, d), lambda i, j: (i, 0)),
            pl.BlockSpec((1, d), const),
            pl.BlockSpec((d, COL), lambda i, j: (0, j)),
            pl.BlockSpec((tm, LANES), lambda i, j: (i % pos_blocks, 0)),
            pl.BlockSpec((tm, LANES), lambda i, j: (i % pos_blocks, 0)),
            pl.BlockSpec((1, COL), const),
            pl.BlockSpec((1, COL), const),
            pl.BlockSpec((MXU_DIM, MXU_DIM), const),
            pl.BlockSpec((MXU_DIM, MXU_DIM), const),
            pl.BlockSpec((1, COL), const),
            pl.BlockSpec(ws.shape, lambda i, j: (0, 0, 0)),
            pl.BlockSpec((CHUNK, COL), const),
        ],
        out_specs=pl.BlockSpec((tm, COL), lambda i, j: (i, jnp.minimum(j, n_out - 1))),
        out_shape=jax.ShapeDtypeStruct((t, n_out * COL), BF16),
        scratch_shapes=[pltpu.VMEM((tm, d), BF16), pltpu.VMEM((tm, COL), F32)],
        compiler_params=pltpu.CompilerParams(
            dimension_semantics=("parallel", "arbitrary"), vmem_limit_bytes=VMEM_LIMIT),
        name="in_proj",
    )(x, g1, w_in, *tabs, gq, gk, seg, perm, gv, ws, bs)


def _attn_kernel(lamv_ref, q_ref, k_ref, v_ref, g_ref, o_ref,
                 qst_sc, vt_sc, s_sc, p_sc, al_sc, m_sc, lead_sc, acc_sc, *, lambda_init):
    i = pl.program_id(2)
    tq = q_ref.shape[0]
    nk, _, tk = vt_sc.shape
    r = 2 * tq

    @pl.when(i == 0)
    def _():
        ones = jnp.ones((BF16_SUBLANES, tk), BF16)
        for c in range(nk):
            vt_sc[c, 0:ATT_VD, :] = v_ref[c * tk:(c + 1) * tk, :].astype(F32).T.astype(BF16)
            vt_sc[c, ATT_VD:, :] = ones

    qt = q_ref[...].astype(F32).T
    row = lax.broadcasted_iota(jnp.int32, qt.shape, 0)
    zero = jnp.zeros_like(qt)
    qst_sc[:, 0:tq] = jnp.where(row < ATT_HD, qt, zero).astype(BF16)
    qst_sc[:, tq:r] = jnp.where(row >= ATT_HD, qt, zero).astype(BF16)
    n_chunks = tk // KEY_CHUNK

    def score_chunk(j, c):
        off = pl.multiple_of(j * tk + c * KEY_CHUNK, KEY_CHUNK)
        return jnp.dot(k_ref[pl.ds(off, KEY_CHUNK), :], qst_sc[...],
                       preferred_element_type=F32)

    def add_values(j, slot):
        acc_sc[...] = al_sc[slot] * (acc_sc[...] + jnp.dot(
            vt_sc[j], p_sc[slot], preferred_element_type=F32))

    m_sc[...] = jnp.max(score_chunk(0, 0), axis=0, keepdims=True)
    lead_sc[...] = jnp.zeros(lead_sc.shape, F32)
    acc_sc[...] = jnp.zeros(acc_sc.shape, F32)
    p_sc[1] = jnp.zeros(p_sc.shape[1:], BF16)
    al_sc[1] = jnp.ones(al_sc.shape[1:], F32)

    def trip(j, slot):
        m_prev = m_sc[...]
        bm = None
        for c in range(n_chunks):
            rows = slice(c * KEY_CHUNK, (c + 1) * KEY_CHUNK)
            s = score_chunk(j, c)
            p_sc[slot, rows, :] = jnp.exp2(s - m_prev).astype(BF16)
            cmx = jnp.max(s, axis=0, keepdims=True)
            bm = cmx if bm is None else jnp.maximum(bm, cmx)
            if c == max(n_chunks // 2 - 1, 0):
                add_values(max(j - 1, 0) if isinstance(j, int) else jnp.maximum(j - 1, 0),
                           1 - slot)
        m_new = jnp.maximum(m_prev, bm)
        m_sc[...] = m_new
        al_sc[slot] = jnp.exp2(m_prev - m_new)
        lead_sc[...] = jnp.maximum(lead_sc[...], bm - m_prev)

    for j0 in range(0, nk, TRIPS_PER_REGION):
        @pl.when(i >= 0)
        def _(j0=j0):
            for j in range(j0, min(j0 + TRIPS_PER_REGION, nk)):
                trip(j, j % 2)
    add_values(nk - 1, (nk - 1) % 2)

    @pl.when(jnp.max(lead_sc[...]) > MAX_LEAD)
    def _():
        m_sc[...] = jnp.full(m_sc.shape, -jnp.inf, F32)
        acc_sc[...] = jnp.zeros(acc_sc.shape, F32)

        def block(j, carry):
            m_prev = m_sc[...]
            for c in range(n_chunks):
                s_sc[c * KEY_CHUNK:(c + 1) * KEY_CHUNK, :] = score_chunk(j, c)
            s = s_sc[...]
            m_new = jnp.maximum(m_prev, jnp.max(s, axis=0, keepdims=True))
            m_sc[...] = m_new
            acc_sc[...] = jnp.exp2(m_prev - m_new) * acc_sc[...] + jnp.dot(
                vt_sc[j], jnp.exp2(s - m_new).astype(BF16), preferred_element_type=F32)
            return carry

        lax.fori_loop(0, nk, block, 0)

    lv = lamv_ref[...]
    lam = (jnp.exp(jnp.sum(lv[0:1] * lv[1:2], axis=1, keepdims=True))
           - jnp.exp(jnp.sum(lv[2:3] * lv[3:4], axis=1, keepdims=True)) + lambda_init)
    o_all = acc_sc[0:ATT_VD, :] / acc_sc[ATT_VD:ATT_VD + 1, :]
    ot = o_all[:, 0:tq] - lam * o_all[:, tq:r]
    ms = jnp.mean(ot * ot, axis=0, keepdims=True)
    o = (ot * lax.rsqrt(ms + EPS)).T * g_ref[...]
    o_ref[...] = (o * (1.0 - lambda_init)).astype(BF16)


def _attention(proj, lamv, g_sub, *, batch, seq, heads, q_col, k_col, v_col, lambda_init, tq, tk):
    nq = seq // tq
    nk = seq // tk
    assert seq % tk == 0 and seq % tq == 0 and tk % KEY_CHUNK == 0
    kern = functools.partial(_attn_kernel, lambda_init=lambda_init)
    return pl.pallas_call(
        kern,
        grid=(batch, heads, nq),
        in_specs=[
            pl.BlockSpec(lamv.shape, lambda b, h, i: (0, 0)),
            pl.BlockSpec((tq, ATT_VD), lambda b, h, i: (b * nq + i, q_col + h)),
            pl.BlockSpec((seq, ATT_VD), lambda b, h, i: (b, k_col + h)),
            pl.BlockSpec((seq, ATT_VD), lambda b, h, i: (b, v_col + h)),
            pl.BlockSpec((1, ATT_VD), lambda b, h, i: (0, 0)),
        ],
        out_specs=pl.BlockSpec((tq, ATT_VD), lambda b, h, i: (b * nq + i, h)),
        out_shape=jax.ShapeDtypeStruct((batch * seq, heads * ATT_VD), BF16),
        scratch_shapes=[
            pltpu.VMEM((ATT_VD, 2 * tq), BF16),
            pltpu.VMEM((nk, ATT_VD + BF16_SUBLANES, tk), BF16),
            pltpu.VMEM((tk, 2 * tq), F32),
            pltpu.VMEM((2, tk, 2 * tq), BF16),
            pltpu.VMEM((2, 1, 2 * tq), F32),
            pltpu.VMEM((1, 2 * tq), F32),
            pltpu.VMEM((1, 2 * tq), F32),
            pltpu.VMEM((ATT_VD + BF16_SUBLANES, 2 * tq), F32),
        ],
        compiler_params=pltpu.CompilerParams(
            dimension_semantics=("arbitrary", "arbitrary", "arbitrary"),
            vmem_limit_bytes=VMEM_LIMIT),
        name="diff_attention",
    )(lamv, proj, proj, proj, g_sub)


def _out_proj_kernel(ax_ref, ab_ref, ac_ref, axp_ref, acp_ref, axn_ref, acn_ref, cw_ref,
                     att_ref, oc_ref, x_ref, w_ref, o_ref, mix_sc, *, pos_blocks):
    i = pl.program_id(0)
    j = pl.program_id(1)
    tm, cc = ax_ref.shape

    @pl.when(j == 0)
    def _():
        z = ac_ref[...].astype(F32) * ax_ref[...].astype(F32)
        last = BF16_SUBLANES - 1
        zp = acp_ref[last:last + 1, :].astype(F32) * axp_ref[last:last + 1, :].astype(F32)
        zn = acn_ref[0:1, :].astype(F32) * axn_ref[0:1, :].astype(F32)
        zp = jnp.where(i % pos_blocks == 0, jnp.zeros_like(zp), zp)
        zn = jnp.where(i % pos_blocks == pos_blocks - 1, jnp.zeros_like(zn), zn)
        row = lax.broadcasted_iota(jnp.int32, z.shape, 0)
        z_prev = jnp.where(row == 0, zp, pltpu.roll(z, 1, 0))
        z_next = jnp.where(row == tm - 1, zn, pltpu.roll(z, tm - 1, 0))
        cw = cw_ref[...]
        conv = cw[0:1] * z_prev + cw[1:2] * z + cw[2:3] * z_next
        mix_sc[:, 0:cc] = (ab_ref[...].astype(F32) * conv).astype(BF16)
        aw = att_ref.shape[1]
        mix_sc[:, cc:cc + aw] = att_ref[...]
        mix_sc[:, cc + aw:] = oc_ref[...]

    o_ref[...] = x_ref[...] + jnp.dot(mix_sc[...], w_ref[...], preferred_element_type=F32)


def _out_proj(proj, att, x, conv_w, w_out, *, seq, tm, tn, oc_col):
    t, d = x.shape
    pos_blocks = seq // tm
    hb = tm // BF16_SUBLANES
    n_hb = t // BF16_SUBLANES
    prev = lambda i: jnp.maximum(i * hb - 1, 0)
    nxt = lambda i: jnp.minimum((i + 1) * hb, n_hb - 1)
    kern = functools.partial(_out_proj_kernel, pos_blocks=pos_blocks)
    return pl.pallas_call(
        kern,
        grid=(t // tm, d // tn),
        in_specs=[
            pl.BlockSpec((tm, COL), lambda i, j: (i, 0)),
            pl.BlockSpec((tm, COL), lambda i, j: (i, 1)),
            pl.BlockSpec((tm, COL), lambda i, j: (i, 2)),
            pl.BlockSpec((BF16_SUBLANES, COL), lambda i, j: (prev(i), 0)),
            pl.BlockSpec((BF16_SUBLANES, COL), lambda i, j: (prev(i), 2)),
            pl.BlockSpec((BF16_SUBLANES, COL), lambda i, j: (nxt(i), 0)),
            pl.BlockSpec((BF16_SUBLANES, COL), lambda i, j: (nxt(i), 2)),
            pl.BlockSpec(conv_w.shape, lambda i, j: (0, 0)),
            pl.BlockSpec((tm, att.shape[1]), lambda i, j: (i, 0)),
            pl.BlockSpec((tm, COL), lambda i, j: (i, oc_col)),
            pl.BlockSpec((tm, tn), lambda i, j: (i, j)),
            pl.BlockSpec((w_out.shape[0], tn), lambda i, j: (0, j)),
        ],
        out_specs=pl.BlockSpec((tm, tn), lambda i, j: (i, j)),
        out_shape=jax.ShapeDtypeStruct((t, d), F32),
        scratch_shapes=[pltpu.VMEM((tm, w_out.shape[0]), BF16)],
        compiler_params=pltpu.CompilerParams(
            dimension_semantics=("parallel", "arbitrary"), vmem_limit_bytes=VMEM_LIMIT),
        name="out_proj",
    )(proj, proj, proj, proj, proj, proj, proj, conv_w, att, proj, x, w_out)


def _ffn_kernel(x_ref, g_ref, wu_ref, wd_ref, o_ref, h_sc):
    f = pl.program_id(1)

    @pl.when(f == 0)
    def _():
        x = x_ref[...]
        h_sc[...] = _rms(x, g_ref[...]).astype(BF16)
        o_ref[...] = x

    hid = jnp.dot(h_sc[...], wu_ref[...], preferred_element_type=F32)
    hid = jnp.square(jnp.maximum(hid, 0.0)).astype(BF16)
    o_ref[...] += jnp.dot(hid, wd_ref[...], preferred_element_type=F32)


def _ffn(x, g2, w_up, w_down, *, tm, tf):
    t, d = x.shape
    ff = w_up.shape[1]
    return pl.pallas_call(
        _ffn_kernel,
        grid=(t // tm, ff // tf),
        in_specs=[
            pl.BlockSpec((tm, d), lambda i, f: (i, 0)),
            pl.BlockSpec((1, d), lambda i, f: (0, 0)),
            pl.BlockSpec((d, tf), lambda i, f: (0, f)),
            pl.BlockSpec((tf, d), lambda i, f: (f, 0)),
        ],
        out_specs=pl.BlockSpec((tm, d), lambda i, f: (i, 0)),
        out_shape=jax.ShapeDtypeStruct((t, d), F32),
        scratch_shapes=[pltpu.VMEM((tm, d), BF16)],
        compiler_params=pltpu.CompilerParams(
            dimension_semantics=("parallel", "arbitrary"), vmem_limit_bytes=VMEM_LIMIT),
        name="ffn",
    )(x, g2, w_up, w_down)


def _rope_tables(seq, rot_dim):
    half = rot_dim // 2
    inv = ROPE_THETA ** (-jnp.arange(0, rot_dim, 2, dtype=F32) / rot_dim)
    ang = jnp.arange(seq, dtype=F32)[:, None] * inv[None, :]
    cos, sin = jnp.cos(ang), jnp.sin(ang)
    pad = ATT_HD - rot_dim
    ones = jnp.ones((seq, pad), F32)
    zeros = jnp.zeros((seq, pad), F32)
    rep = LANES // ATT_HD
    cos_t = jnp.tile(jnp.concatenate([cos, cos, ones], axis=1), (1, rep))
    sin_t = jnp.tile(jnp.concatenate([-sin, sin, zeros], axis=1), (1, rep))
    return cos_t, sin_t


def _prep_layer(l, norm1_g, w_in, conv_w, q_norm_g, k_norm_g, lam_q1, lam_k1, lam_q2, lam_k2,
                subln_g, sgu_norm_g, sgu_w, sgu_b, w_out, norm2_g, w_up, w_down):
    d = w_in.shape[1]
    mlp_heads = sgu_w.shape[1]
    grp = jnp.arange(MXU_DIM) // ATT_HD
    seg = jnp.where(grp[:, None] == grp[None, :], 1.0 / ATT_HD, 0.0).astype(BF16)
    lane = jnp.arange(MXU_DIM) % ATT_HD
    half = ATT_HD // 8
    partner = jnp.arange(MXU_DIM) + jnp.where(lane < half, half, -half)
    perm = ((jnp.arange(MXU_DIM)[:, None] == partner[None, :])
            & (lane[None, :] < 2 * half)).astype(BF16)
    return dict(
        g1=norm1_g[l].reshape(1, d),
        w_in=w_in[l].astype(BF16),
        conv_w=conv_w[l],
        gq=jnp.tile(q_norm_g[l], COL // ATT_HD).reshape(1, COL),
        gk=jnp.tile(k_norm_g[l], COL // ATT_HD).reshape(1, COL),
        seg=seg,
        perm=perm,
        lamv=jnp.stack([lam_q1[l], lam_k1[l], lam_q2[l], lam_k2[l]]).astype(F32),
        g_sub=subln_g[l].reshape(1, ATT_VD),
        gv=jnp.tile(sgu_norm_g[l], mlp_heads).reshape(1, mlp_heads * MLP_HD),
        ws=sgu_w[l].astype(BF16),
        bs=jnp.repeat(sgu_b[l].T, MLP_HD, axis=1),
        w_out=w_out[l].astype(BF16),
        g2=norm2_g[l].reshape(1, d),
        w_up=w_up[l].astype(BF16),
        w_down=w_down[l].astype(BF16),
        lambda_init=0.8 - 0.6 * math.exp(-0.3 * l),
    )


def _trunk(x, layers, tm=1024, tq=512, tk=512, tn=512, tf=512):
    b, s, d = x.shape
    t = b * s
    heads = (d // 2) // ATT_VD
    tabs = _rope_tables(s, ATT_HD // 4)
    xf = x.reshape(t, d)
    for p in layers:
        proj = _in_proj(xf, p["g1"], p["w_in"], tabs, p["gq"], p["gk"], p["seg"], p["perm"],
                        p["gv"],
                        p["ws"], p["bs"], seq=s, tm=tm)
        att = _attention(proj, p["lamv"], p["g_sub"], batch=b, seq=s, heads=heads,
                         q_col=12, k_col=12 + heads, v_col=12 + 2 * heads,
                         lambda_init=p["lambda_init"], tq=tq, tk=tk)
        x1 = _out_proj(proj, att, xf, p["conv_w"], p["w_out"], seq=s, tm=tm, tn=tn, oc_col=9)
        xf = _ffn(x1, p["g2"], p["w_up"], p["w_down"], tm=tm, tf=tf)
    return xf.reshape(b, s, d)


def kernel(x_prompt, x_sample, norm1_g, w_in, conv_w, q_norm_g, k_norm_g, lam_q1, lam_k1, lam_q2,
           lam_k2, subln_g, sgu_norm_g, sgu_w, sgu_b, w_out, norm2_g, w_up, w_down):
    params = (norm1_g, w_in, conv_w, q_norm_g, k_norm_g, lam_q1, lam_k1, lam_q2, lam_k2,
              subln_g, sgu_norm_g, sgu_w, sgu_b, w_out, norm2_g, w_up, w_down)
    layers = [_prep_layer(l, *params) for l in range(w_in.shape[0])]
    return (_trunk(x_prompt, layers), _trunk(x_sample, layers))
```

```python
import functools
import math

import jax
import jax.numpy as jnp
from jax import lax
from jax.experimental import pallas as pl
from jax.experimental.pallas import tpu as pltpu

F32 = jnp.float32
BF16 = jnp.bfloat16

EPS = 1e-6
ROPE_THETA = 500000.0
CONV_WIDTH = 3
ATT_HD = 64
ATT_VD = 2 * ATT_HD
KEY_CHUNK = 128
MAX_LEAD = 64.0
CHUNK = 128
MLP_HD = 128
LANES = 128
MXU_DIM = 256
BF16_SUBLANES = 16
COL = 512
VMEM_LIMIT = 56 * 1024 * 1024
LOG2E = math.log2(math.e)


def _rms(x, gain):
    ms = jnp.mean(x * x, axis=-1, keepdims=True)
    return x * lax.rsqrt(ms + EPS) * gain


def _in_proj_kernel(x_ref, g1_ref, w_ref, cos_ref, sin_ref, gq_ref, gk_ref, seg_ref, perm_ref,
                    gv_ref, ws_ref, bs_ref, o_ref, h_sc, u_sc, *, n_conv, n_qk, n_v):
    j = pl.program_id(1)
    tm = x_ref.shape[0]

    @pl.when(j == 0)
    def _():
        h_sc[...] = _rms(x_ref[...], g1_ref[...]).astype(BF16)

    acc = jnp.dot(h_sc[...], w_ref[...], preferred_element_type=F32)

    q_lo, k_lo = n_conv, n_conv + n_qk
    v_lo, u_at = k_lo + n_qk, k_lo + n_qk + n_v

    def qk_norm_rope(gain):
        sq = (acc * acc).astype(BF16)
        ms = jnp.concatenate(
            [jnp.dot(sq[:, c:c + MXU_DIM], seg_ref[...], preferred_element_type=F32)
             for c in range(0, COL, MXU_DIM)], axis=1)
        y = acc * lax.rsqrt(ms + EPS) * gain
        yb = y.astype(BF16)
        cos, sin = cos_ref[...], sin_ref[...]
        for c in range(0, COL, MXU_DIM):
            rot = jnp.dot(yb[:, c:c + MXU_DIM], perm_ref[...], preferred_element_type=F32)
            for cc in range(c, c + MXU_DIM, LANES):
                r = y[:, cc:cc + LANES] * cos + rot[:, cc - c:cc - c + LANES] * sin
                o_ref[:, cc:cc + LANES] = r.astype(BF16)

    @pl.when((j < q_lo) | ((j >= v_lo) & (j < u_at)))
    def _():
        o_ref[...] = acc.astype(BF16)

    @pl.when((j >= q_lo) & (j < k_lo))
    def _():
        qk_norm_rope(gq_ref[...] * (ATT_HD ** -0.5 * LOG2E))

    @pl.when((j >= k_lo) & (j < v_lo))
    def _():
        qk_norm_rope(gk_ref[...])

    @pl.when(j == u_at)
    def _():
        u_sc[...] = jax.nn.gelu(acc)

    @pl.when(j == u_at + 1)
    def _():
        v = jax.nn.gelu(acc)
        for h in range(COL // MLP_HD):
            sl = slice(h * MLP_HD, (h + 1) * MLP_HD)
            vn = _rms(v[:, sl], gv_ref[:, sl]).astype(BF16)
            w = ws_ref[h]
            for n in range(tm // CHUNK):
                rows = slice(n * CHUNK, (n + 1) * CHUNK)
                mixed = jnp.dot(w, vn[rows, :], preferred_element_type=F32) + bs_ref[:, sl]
                o_ref[rows, sl] = (u_sc[rows, sl] * mixed).astype(BF16)


def _in_proj(x, g1, w_in, tabs, gq, gk, seg, perm, gv, ws, bs, *, seq, tm):
    t, d = x.shape
    n_in = w_in.shape[1] // COL
    n_out = n_in - 1
    pos_blocks = seq // tm
    const = lambda i, j: (0, 0)
    kern = functools.partial(_in_proj_kernel, n_conv=3, n_qk=2, n_v=2)
    return pl.pallas_call(
        kern,
        grid=(t // tm, n_in),
        in_specs=[
            pl.BlockSpec((tm, d), lambda i, j: (i, 0)),
            pl.BlockSpec((1, d), const),
            pl.BlockSpec((d, COL), lambda i, j: (0, j)),
            pl.BlockSpec((tm, LANES), lambda i, j: (i % pos_blocks, 0)),
            pl.BlockSpec((tm, LANES), lambda i, j: (i % pos_blocks, 0)),
            pl.BlockSpec((1, COL), const),
            pl.BlockSpec((1, COL), const),
            pl.BlockSpec((MXU_DIM, MXU_DIM), const),
            pl.BlockSpec((MXU_DIM, MXU_DIM), const),
            pl.BlockSpec((1, COL), const),
            pl.BlockSpec(ws.shape, lambda i, j: (0, 0, 0)),
            pl.BlockSpec((CHUNK, COL), const),
        ],
        out_specs=pl.BlockSpec((tm, COL), lambda i, j: (i, jnp.minimum(j, n_out - 1))),
        out_shape=jax.ShapeDtypeStruct((t, n_out * COL), BF16),
        scratch_shapes=[pltpu.VMEM((tm, d), BF16), pltpu.VMEM((tm, COL), F32)],
        compiler_params=pltpu.CompilerParams(
            dimension_semantics=("parallel", "arbitrary"), vmem_limit_bytes=VMEM_LIMIT),
        name="in_proj",
    )(x, g1, w_in, *tabs, gq, gk, seg, perm, gv, ws, bs)


def _attn_kernel(lamv_ref, q_ref, k_ref, v_ref, g_ref, o_ref,
                 qst_sc, vt_sc, s_sc, p_sc, al_sc, m_sc, lead_sc, acc_sc, *, lambda_init):
    i = pl.program_id(2)
    tq = q_ref.shape[0]
    nk, _, tk = vt_sc.shape
    r = 2 * tq

    @pl.when(i == 0)
    def _():
        ones = jnp.ones((BF16_SUBLANES, tk), BF16)
        for c in range(nk):
            vt_sc[c, 0:ATT_VD, :] = v_ref[c * tk:(c + 1) * tk, :].astype(F32).T.astype(BF16)
            vt_sc[c, ATT_VD:, :] = ones

    qt = q_ref[...].astype(F32).T
    row = lax.broadcasted_iota(jnp.int32, qt.shape, 0)
    zero = jnp.zeros_like(qt)
    qst_sc[:, 0:tq] = jnp.where(row < ATT_HD, qt, zero).astype(BF16)
    qst_sc[:, tq:r] = jnp.where(row >= ATT_HD, qt, zero).astype(BF16)
    n_chunks = tk // KEY_CHUNK

    def score_chunk(j, c):
        off = pl.multiple_of(j * tk + c * KEY_CHUNK, KEY_CHUNK)
        return jnp.dot(k_ref[pl.ds(off, KEY_CHUNK), :], qst_sc[...],
                       preferred_element_type=F32)

    def add_values(j, slot):
        acc_sc[...] = al_sc[slot] * (acc_sc[...] + jnp.dot(
            vt_sc[j], p_sc[slot], preferred_element_type=F32))

    m_sc[...] = jnp.max(score_chunk(0, 0), axis=0, keepdims=True)
    lead_sc[...] = jnp.zeros(lead_sc.shape, F32)
    acc_sc[...] = jnp.zeros(acc_sc.shape, F32)
    p_sc[1] = jnp.zeros(p_sc.shape[1:], BF16)
    al_sc[1] = jnp.ones(al_sc.shape[1:], F32)

    def trip(j, slot):
        m_prev = m_sc[...]
        bm = None
        for c in range(n_chunks):
            rows = slice(c * KEY_CHUNK, (c + 1) * KEY_CHUNK)
            s = score_chunk(j, c)
            p_sc[slot, rows, :] = jnp.exp2(s - m_prev).astype(BF16)
            cmx = jnp.max(s, axis=0, keepdims=True)
            bm = cmx if bm is None else jnp.maximum(bm, cmx)
            if c == max(n_chunks // 2 - 1, 0):
                add_values(max(j - 1, 0) if isinstance(j, int) else jnp.maximum(j - 1, 0),
                           1 - slot)
        m_new = jnp.maximum(m_prev, bm)
        m_sc[...] = m_new
        al_sc[slot] = jnp.exp2(m_prev - m_new)
        lead_sc[...] = jnp.maximum(lead_sc[...], bm - m_prev)

    for j in range(nk):
        trip(j, j % 2)
    add_values(nk - 1, (nk - 1) % 2)

    def write_output():
        lv = lamv_ref[...]
        lam = (jnp.exp(jnp.sum(lv[0:1] * lv[1:2], axis=1, keepdims=True))
               - jnp.exp(jnp.sum(lv[2:3] * lv[3:4], axis=1, keepdims=True)) + lambda_init)
        o_all = acc_sc[0:ATT_VD, :] / acc_sc[ATT_VD:ATT_VD + 1, :]
        ot = o_all[:, 0:tq] - lam * o_all[:, tq:r]
        ms = jnp.mean(ot * ot, axis=0, keepdims=True)
        o = (ot * lax.rsqrt(ms + EPS)).T * g_ref[...]
        o_ref[...] = (o * (1.0 - lambda_init)).astype(BF16)

    write_output()

    @pl.when(jnp.max(lead_sc[...]) > MAX_LEAD)
    def _():
        m_sc[...] = jnp.full(m_sc.shape, -jnp.inf, F32)
        acc_sc[...] = jnp.zeros(acc_sc.shape, F32)

        def block(j, carry):
            m_prev = m_sc[...]
            for c in range(n_chunks):
                s_sc[c * KEY_CHUNK:(c + 1) * KEY_CHUNK, :] = score_chunk(j, c)
            s = s_sc[...]
            m_new = jnp.maximum(m_prev, jnp.max(s, axis=0, keepdims=True))
            m_sc[...] = m_new
            acc_sc[...] = jnp.exp2(m_prev - m_new) * acc_sc[...] + jnp.dot(
                vt_sc[j], jnp.exp2(s - m_new).astype(BF16), preferred_element_type=F32)
            return carry

        lax.fori_loop(0, nk, block, 0)
        write_output()


def _attention(proj, lamv, g_sub, *, batch, seq, heads, q_col, k_col, v_col, lambda_init, tq, tk):
    nq = seq // tq
    nk = seq // tk
    assert seq % tk == 0 and seq % tq == 0 and tk % KEY_CHUNK == 0
    kern = functools.partial(_attn_kernel, lambda_init=lambda_init)
    return pl.pallas_call(
        kern,
        grid=(batch, heads, nq),
        in_specs=[
            pl.BlockSpec(lamv.shape, lambda b, h, i: (0, 0)),
            pl.BlockSpec((tq, ATT_VD), lambda b, h, i: (b * nq + i, q_col + h)),
            pl.BlockSpec((seq, ATT_VD), lambda b, h, i: (b, k_col + h)),
            pl.BlockSpec((seq, ATT_VD), lambda b, h, i: (b, v_col + h)),
            pl.BlockSpec((1, ATT_VD), lambda b, h, i: (0, 0)),
        ],
        out_specs=pl.BlockSpec((tq, ATT_VD), lambda b, h, i: (b * nq + i, h)),
        out_shape=jax.ShapeDtypeStruct((batch * seq, heads * ATT_VD), BF16),
        scratch_shapes=[
            pltpu.VMEM((ATT_VD, 2 * tq), BF16),
            pltpu.VMEM((nk, ATT_VD + BF16_SUBLANES, tk), BF16),
            pltpu.VMEM((tk, 2 * tq), F32),
            pltpu.VMEM((2, tk, 2 * tq), BF16),
            pltpu.VMEM((2, 1, 2 * tq), F32),
            pltpu.VMEM((1, 2 * tq), F32),
            pltpu.VMEM((1, 2 * tq), F32),
            pltpu.VMEM((ATT_VD + BF16_SUBLANES, 2 * tq), F32),
        ],
        compiler_params=pltpu.CompilerParams(
            dimension_semantics=("arbitrary", "arbitrary", "arbitrary"),
            vmem_limit_bytes=VMEM_LIMIT),
        name="diff_attention",
    )(lamv, proj, proj, proj, g_sub)


def _out_proj_kernel(ax_ref, ab_ref, ac_ref, axp_ref, acp_ref, axn_ref, acn_ref, cw_ref,
                     att_ref, oc_ref, x_ref, w_ref, o_ref, mix_sc, *, pos_blocks):
    i = pl.program_id(0)
    j = pl.program_id(1)
    tm, cc = ax_ref.shape

    @pl.when(j == 0)
    def _():
        z = ac_ref[...].astype(F32) * ax_ref[...].astype(F32)
        last = BF16_SUBLANES - 1
        zp = acp_ref[last:last + 1, :].astype(F32) * axp_ref[last:last + 1, :].astype(F32)
        zn = acn_ref[0:1, :].astype(F32) * axn_ref[0:1, :].astype(F32)
        zp = jnp.where(i % pos_blocks == 0, jnp.zeros_like(zp), zp)
        zn = jnp.where(i % pos_blocks == pos_blocks - 1, jnp.zeros_like(zn), zn)
        row = lax.broadcasted_iota(jnp.int32, z.shape, 0)
        z_prev = jnp.where(row == 0, zp, pltpu.roll(z, 1, 0))
        z_next = jnp.where(row == tm - 1, zn, pltpu.roll(z, tm - 1, 0))
        cw = cw_ref[...]
        conv = cw[0:1] * z_prev + cw[1:2] * z + cw[2:3] * z_next
        mix_sc[:, 0:cc] = (ab_ref[...].astype(F32) * conv).astype(BF16)
        aw = att_ref.shape[1]
        mix_sc[:, cc:cc + aw] = att_ref[...]
        mix_sc[:, cc + aw:] = oc_ref[...]

    o_ref[...] = x_ref[...] + jnp.dot(mix_sc[...], w_ref[...], preferred_element_type=F32)


def _out_proj(proj, att, x, conv_w, w_out, *, seq, tm, tn, oc_col):
    t, d = x.shape
    pos_blocks = seq // tm
    hb = tm // BF16_SUBLANES
    n_hb = t // BF16_SUBLANES
    prev = lambda i: jnp.maximum(i * hb - 1, 0)
    nxt = lambda i: jnp.minimum((i + 1) * hb, n_hb - 1)
    kern = functools.partial(_out_proj_kernel, pos_blocks=pos_blocks)
    return pl.pallas_call(
        kern,
        grid=(t // tm, d // tn),
        in_specs=[
            pl.BlockSpec((tm, COL), lambda i, j: (i, 0)),
            pl.BlockSpec((tm, COL), lambda i, j: (i, 1)),
            pl.BlockSpec((tm, COL), lambda i, j: (i, 2)),
            pl.BlockSpec((BF16_SUBLANES, COL), lambda i, j: (prev(i), 0)),
            pl.BlockSpec((BF16_SUBLANES, COL), lambda i, j: (prev(i), 2)),
            pl.BlockSpec((BF16_SUBLANES, COL), lambda i, j: (nxt(i), 0)),
            pl.BlockSpec((BF16_SUBLANES, COL), lambda i, j: (nxt(i), 2)),
            pl.BlockSpec(conv_w.shape, lambda i, j: (0, 0)),
            pl.BlockSpec((tm, att.shape[1]), lambda i, j: (i, 0)),
            pl.BlockSpec((tm, COL), lambda i, j: (i, oc_col)),
            pl.BlockSpec((tm, tn), lambda i, j: (i, j)),
            pl.BlockSpec((w_out.shape[0], tn), lambda i, j: (0, j)),
        ],
        out_specs=pl.BlockSpec((tm, tn), lambda i, j: (i, j)),
        out_shape=jax.ShapeDtypeStruct((t, d), F32),
        scratch_shapes=[pltpu.VMEM((tm, w_out.shape[0]), BF16)],
        compiler_params=pltpu.CompilerParams(
            dimension_semantics=("parallel", "arbitrary"), vmem_limit_bytes=VMEM_LIMIT),
        name="out_proj",
    )(proj, proj, proj, proj, proj, proj, proj, conv_w, att, proj, x, w_out)


def _ffn_kernel(x_ref, g_ref, wu_ref, wd_ref, o_ref, h_sc):
    f = pl.program_id(1)

    @pl.when(f == 0)
    def _():
        x = x_ref[...]
        h_sc[...] = _rms(x, g_ref[...]).astype(BF16)
        o_ref[...] = x

    hid = jnp.dot(h_sc[...], wu_ref[...], preferred_element_type=F32)
    hid = jnp.square(jnp.maximum(hid, 0.0)).astype(BF16)
    o_ref[...] += jnp.dot(hid, wd_ref[...], preferred_element_type=F32)


def _ffn(x, g2, w_up, w_down, *, tm, tf):
    t, d = x.shape
    ff = w_up.shape[1]
    return pl.pallas_call(
        _ffn_kernel,
        grid=(t // tm, ff // tf),
        in_specs=[
            pl.BlockSpec((tm, d), lambda i, f: (i, 0)),
            pl.BlockSpec((1, d), lambda i, f: (0, 0)),
            pl.BlockSpec((d, tf), lambda i, f: (0, f)),
            pl.BlockSpec((tf, d), lambda i, f: (f, 0)),
        ],
        out_specs=pl.BlockSpec((tm, d), lambda i, f: (i, 0)),
        out_shape=jax.ShapeDtypeStruct((t, d), F32),
        scratch_shapes=[pltpu.VMEM((tm, d), BF16)],
        compiler_params=pltpu.CompilerParams(
            dimension_semantics=("parallel", "arbitrary"), vmem_limit_bytes=VMEM_LIMIT),
        name="ffn",
    )(x, g2, w_up, w_down)


def _rope_tables(seq, rot_dim):
    half = rot_dim // 2
    inv = ROPE_THETA ** (-jnp.arange(0, rot_dim, 2, dtype=F32) / rot_dim)
    ang = jnp.arange(seq, dtype=F32)[:, None] * inv[None, :]
    cos, sin = jnp.cos(ang), jnp.sin(ang)
    pad = ATT_HD - rot_dim
    ones = jnp.ones((seq, pad), F32)
    zeros = jnp.zeros((seq, pad), F32)
    rep = LANES // ATT_HD
    cos_t = jnp.tile(jnp.concatenate([cos, cos, ones], axis=1), (1, rep))
    sin_t = jnp.tile(jnp.concatenate([-sin, sin, zeros], axis=1), (1, rep))
    return cos_t, sin_t


def _prep_layer(l, norm1_g, w_in, conv_w, q_norm_g, k_norm_g, lam_q1, lam_k1, lam_q2, lam_k2,
                subln_g, sgu_norm_g, sgu_w, sgu_b, w_out, norm2_g, w_up, w_down):
    d = w_in.shape[1]
    mlp_heads = sgu_w.shape[1]
    grp = jnp.arange(MXU_DIM) // ATT_HD
    seg = jnp.where(grp[:, None] == grp[None, :], 1.0 / ATT_HD, 0.0).astype(BF16)
    lane = jnp.arange(MXU_DIM) % ATT_HD
    half = ATT_HD // 8
    partner = jnp.arange(MXU_DIM) + jnp.where(lane < half, half, -half)
    perm = ((jnp.arange(MXU_DIM)[:, None] == partner[None, :])
            & (lane[None, :] < 2 * half)).astype(BF16)
    return dict(
        g1=norm1_g[l].reshape(1, d),
        w_in=w_in[l].astype(BF16),
        conv_w=conv_w[l],
        gq=jnp.tile(q_norm_g[l], COL // ATT_HD).reshape(1, COL),
        gk=jnp.tile(k_norm_g[l], COL // ATT_HD).reshape(1, COL),
        seg=seg,
        perm=perm,
        lamv=jnp.stack([lam_q1[l], lam_k1[l], lam_q2[l], lam_k2[l]]).astype(F32),
        g_sub=subln_g[l].reshape(1, ATT_VD),
        gv=jnp.tile(sgu_norm_g[l], mlp_heads).reshape(1, mlp_heads * MLP_HD),
        ws=sgu_w[l].astype(BF16),
        bs=jnp.repeat(sgu_b[l].T, MLP_HD, axis=1),
        w_out=w_out[l].astype(BF16),
        g2=norm2_g[l].reshape(1, d),
        w_up=w_up[l].astype(BF16),
        w_down=w_down[l].astype(BF16),
        lambda_init=0.8 - 0.6 * math.exp(-0.3 * l),
    )


def _trunk(x, layers, tm=1024, tq=512, tk=512, tn=512, tf=512):
    b, s, d = x.shape
    t = b * s
    heads = (d // 2) // ATT_VD
    tabs = _rope_tables(s, ATT_HD // 4)
    xf = x.reshape(t, d)
    for p in layers:
        proj = _in_proj(xf, p["g1"], p["w_in"], tabs, p["gq"], p["gk"], p["seg"], p["perm"],
                        p["gv"],
                        p["ws"], p["bs"], seq=s, tm=tm)
        att = _attention(proj, p["lamv"], p["g_sub"], batch=b, seq=s, heads=heads,
                         q_col=12, k_col=12 + heads, v_col=12 + 2 * heads,
                         lambda_init=p["lambda_init"], tq=tq, tk=tk)
        x1 = _out_proj(proj, att, xf, p["conv_w"], p["w_out"], seq=s, tm=tm, tn=tn, oc_col=9)
        xf = _ffn(x1, p["g2"], p["w_up"], p["w_down"], tm=tm, tf=tf)
    return xf.reshape(b, s, d)


def kernel(x_prompt, x_sample, norm1_g, w_in, conv_w, q_norm_g, k_norm_g, lam_q1, lam_k1, lam_q2,
           lam_k2, subln_g, sgu_norm_g, sgu_w, sgu_b, w_out, norm2_g, w_up, w_down):
    params = (norm1_g, w_in, conv_w, q_norm_g, k_norm_g, lam_q1, lam_k1, lam_q2, lam_k2,
              subln_g, sgu_norm_g, sgu_w, sgu_b, w_out, norm2_g, w_up, w_down)
    layers = [_prep_layer(l, *params) for l in range(w_in.shape[0])]
    return (_trunk(x_prompt, layers), _trunk(x_sample, layers))
```

```python
import functools
import math

import jax
import jax.numpy as jnp
from jax import lax
from jax.experimental import pallas as pl
from jax.experimental.pallas import tpu as pltpu

F32 = jnp.float32
BF16 = jnp.bfloat16

EPS = 1e-6
ROPE_THETA = 500000.0
CONV_WIDTH = 3
ATT_HD = 64
ATT_VD = 2 * ATT_HD
KEY_CHUNK = 128
MAX_LEAD = 64.0
CHUNK = 128
MLP_HD = 128
LANES = 128
MXU_DIM = 256
BF16_SUBLANES = 16
COL = 512
VMEM_LIMIT = 56 * 1024 * 1024
LOG2E = math.log2(math.e)


def _rms(x, gain):
    ms = jnp.mean(x * x, axis=-1, keepdims=True)
    return x * lax.rsqrt(ms + EPS) * gain


def _in_proj_kernel(x_ref, g1_ref, w_ref, cos_ref, sin_ref, gq_ref, gk_ref, seg_ref, perm_ref,
                    gv_ref, ws_ref, bs_ref, o_ref, h_sc, u_sc, *, n_conv, n_qk, n_v):
    j = pl.program_id(1)
    tm = x_ref.shape[0]

    @pl.when(j == 0)
    def _():
        h_sc[...] = _rms(x_ref[...], g1_ref[...]).astype(BF16)

    acc = jnp.dot(h_sc[...], w_ref[...], preferred_element_type=F32)

    q_lo, k_lo = n_conv, n_conv + n_qk
    v_lo, u_at = k_lo + n_qk, k_lo + n_qk + n_v

    def qk_norm_rope(gain):
        sq = (acc * acc).astype(BF16)
        ms = jnp.concatenate(
            [jnp.dot(sq[:, c:c + MXU_DIM], seg_ref[...], preferred_element_type=F32)
             for c in range(0, COL, MXU_DIM)], axis=1)
        y = acc * lax.rsqrt(ms + EPS) * gain
        yb = y.astype(BF16)
        cos, sin = cos_ref[...], sin_ref[...]
        for c in range(0, COL, MXU_DIM):
            rot = jnp.dot(yb[:, c:c + MXU_DIM], perm_ref[...], preferred_element_type=F32)
            for cc in range(c, c + MXU_DIM, LANES):
                r = y[:, cc:cc + LANES] * cos + rot[:, cc - c:cc - c + LANES] * sin
                o_ref[:, cc:cc + LANES] = r.astype(BF16)

    @pl.when((j < q_lo) | ((j >= v_lo) & (j < u_at)))
    def _():
        o_ref[...] = acc.astype(BF16)

    @pl.when((j >= q_lo) & (j < k_lo))
    def _():
        qk_norm_rope(gq_ref[...] * (ATT_HD ** -0.5 * LOG2E))

    @pl.when((j >= k_lo) & (j < v_lo))
    def _():
        qk_norm_rope(gk_ref[...])

    @pl.when(j == u_at)
    def _():
        u_sc[...] = jax.nn.gelu(acc)

    @pl.when(j == u_at + 1)
    def _():
        v = jax.nn.gelu(acc)
        for h in range(COL // MLP_HD):
            sl = slice(h * MLP_HD, (h + 1) * MLP_HD)
            vn = _rms(v[:, sl], gv_ref[:, sl]).astype(BF16)
            w = ws_ref[h]
            for n in range(tm // CHUNK):
                rows = slice(n * CHUNK, (n + 1) * CHUNK)
                mixed = jnp.dot(w, vn[rows, :], preferred_element_type=F32) + bs_ref[:, sl]
                o_ref[rows, sl] = (u_sc[rows, sl] * mixed).astype(BF16)


def _in_proj(x, g1, w_in, tabs, gq, gk, seg, perm, gv, ws, bs, *, seq, tm):
    t, d = x.shape
    n_in = w_in.shape[1] // COL
    n_out = n_in - 1
    pos_blocks = seq // tm
    const = lambda i, j: (0, 0)
    kern = functools.partial(_in_proj_kernel, n_conv=3, n_qk=2, n_v=2)
    return pl.pallas_call(
        kern,
        grid=(t // tm, n_in),
        in_specs=[
            pl.BlockSpec((tm, d), lambda i, j: (i, 0)),
            pl.BlockSpec((1, d), const),
            pl.BlockSpec((d, COL), lambda i, j: (0, j)),
            pl.BlockSpec((tm, LANES), lambda i, j: (i % pos_blocks, 0)),
            pl.BlockSpec((tm, LANES), lambda i, j: (i % pos_blocks, 0)),
            pl.BlockSpec((1, COL), const),
            pl.BlockSpec((1, COL), const),
            pl.BlockSpec((MXU_DIM, MXU_DIM), const),
            pl.BlockSpec((MXU_DIM, MXU_DIM), const),
            pl.BlockSpec((1, COL), const),
            pl.BlockSpec(ws.shape, lambda i, j: (0, 0, 0)),
            pl.BlockSpec((CHUNK, COL), const),
        ],
        out_specs=pl.BlockSpec((tm, COL), lambda i, j: (i, jnp.minimum(j, n_out - 1))),
        out_shape=jax.ShapeDtypeStruct((t, n_out * COL), BF16),
        scratch_shapes=[pltpu.VMEM((tm, d), BF16), pltpu.VMEM((tm, COL), F32)],
        compiler_params=pltpu.CompilerParams(
            dimension_semantics=("parallel", "arbitrary"), vmem_limit_bytes=VMEM_LIMIT),
        name="in_proj",
    )(x, g1, w_in, *tabs, gq, gk, seg, perm, gv, ws, bs)


def _attn_kernel(lamv_ref, q_ref, k_ref, v_ref, g_ref, o_ref,
                 qst_sc, vt_sc, s_sc, p_sc, al_sc, m_sc, lead_sc, acc_sc, *, lambda_init):
    i = pl.program_id(2)
    tq = q_ref.shape[0]
    nk, _, tk = vt_sc.shape
    r = 2 * tq

    @pl.when(i == 0)
    def _():
        ones = jnp.ones((BF16_SUBLANES, tk), BF16)
        for c in range(nk):
            vt_sc[c, 0:ATT_VD, :] = v_ref[c * tk:(c + 1) * tk, :].astype(F32).T.astype(BF16)
            vt_sc[c, ATT_VD:, :] = ones

    qt = q_ref[...].astype(F32).T
    row = lax.broadcasted_iota(jnp.int32, qt.shape, 0)
    zero = jnp.zeros_like(qt)
    qst_sc[:, 0:tq] = jnp.where(row < ATT_HD, qt, zero).astype(BF16)
    qst_sc[:, tq:r] = jnp.where(row >= ATT_HD, qt, zero).astype(BF16)
    n_chunks = tk // KEY_CHUNK

    def score_chunk(j, c):
        off = pl.multiple_of(j * tk + c * KEY_CHUNK, KEY_CHUNK)
        return jnp.dot(k_ref[pl.ds(off, KEY_CHUNK), :], qst_sc[...],
                       preferred_element_type=F32)

    def add_values(j, slot):
        acc_sc[...] = al_sc[slot] * (acc_sc[...] + jnp.dot(
            vt_sc[j], p_sc[slot], preferred_element_type=F32))

    m_sc[...] = jnp.max(score_chunk(0, 0), axis=0, keepdims=True)
    lead_sc[...] = jnp.zeros(lead_sc.shape, F32)
    acc_sc[...] = jnp.zeros(acc_sc.shape, F32)
    p_sc[1] = jnp.zeros(p_sc.shape[1:], BF16)
    al_sc[1] = jnp.ones(al_sc.shape[1:], F32)

    def trip(j, slot):
        m_prev = m_sc[...]
        bm = None
        for c in range(n_chunks):
            rows = slice(c * KEY_CHUNK, (c + 1) * KEY_CHUNK)
            s = score_chunk(j, c)
            p_sc[slot, rows, :] = jnp.exp2(s - m_prev).astype(BF16)
            cmx = jnp.max(s, axis=0, keepdims=True)
            bm = cmx if bm is None else jnp.maximum(bm, cmx)
            if c == max(n_chunks // 2 - 1, 0):
                add_values(max(j - 1, 0) if isinstance(j, int) else jnp.maximum(j - 1, 0),
                           1 - slot)
        m_new = jnp.maximum(m_prev, bm)
        m_sc[...] = m_new
        al_sc[slot] = jnp.exp2(m_prev - m_new)
        lead_sc[...] = jnp.maximum(lead_sc[...], bm - m_prev)

    for j in range(nk):
        trip(j, j % 2)
    add_values(nk - 1, (nk - 1) % 2)

    def write_output():
        lv = lamv_ref[...]
        lam = (jnp.exp(jnp.sum(lv[0:1] * lv[1:2], axis=1, keepdims=True))
               - jnp.exp(jnp.sum(lv[2:3] * lv[3:4], axis=1, keepdims=True)) + lambda_init)
        o_all = acc_sc[0:ATT_VD, :] / acc_sc[ATT_VD:ATT_VD + 1, :]
        ot = o_all[:, 0:tq] - lam * o_all[:, tq:r]
        ms = jnp.mean(ot * ot, axis=0, keepdims=True)
        o = (ot * lax.rsqrt(ms + EPS)).T * g_ref[...]
        o_ref[...] = (o * (1.0 - lambda_init)).astype(BF16)

    write_output()

    @pl.when(jnp.max(lead_sc[...]) > MAX_LEAD)
    def _():
        m_sc[...] = jnp.full(m_sc.shape, -jnp.inf, F32)
        acc_sc[...] = jnp.zeros(acc_sc.shape, F32)

        def block(j, carry):
            m_prev = m_sc[...]
            for c in range(n_chunks):
                s_sc[c * KEY_CHUNK:(c + 1) * KEY_CHUNK, :] = score_chunk(j, c)
            s = s_sc[...]
            m_new = jnp.maximum(m_prev, jnp.max(s, axis=0, keepdims=True))
            m_sc[...] = m_new
            acc_sc[...] = jnp.exp2(m_prev - m_new) * acc_sc[...] + jnp.dot(
                vt_sc[j], jnp.exp2(s - m_new).astype(BF16), preferred_element_type=F32)
            return carry

        lax.fori_loop(0, nk, block, 0)
        write_output()


def _attention(proj, lamv, g_sub, *, batch, seq, heads, q_col, k_col, v_col, lambda_init, tq, tk):
    nq = seq // tq
    nk = seq // tk
    assert seq % tk == 0 and seq % tq == 0 and tk % KEY_CHUNK == 0
    kern = functools.partial(_attn_kernel, lambda_init=lambda_init)
    return pl.pallas_call(
        kern,
        grid=(batch, heads, nq),
        in_specs=[
            pl.BlockSpec(lamv.shape, lambda b, h, i: (0, 0)),
            pl.BlockSpec((tq, ATT_VD), lambda b, h, i: (b * nq + i, q_col + h)),
            pl.BlockSpec((seq, ATT_VD), lambda b, h, i: (b, k_col + h)),
            pl.BlockSpec((seq, ATT_VD), lambda b, h, i: (b, v_col + h)),
            pl.BlockSpec((1, ATT_VD), lambda b, h, i: (0, 0)),
        ],
        out_specs=pl.BlockSpec((tq, ATT_VD), lambda b, h, i: (b * nq + i, h)),
        out_shape=jax.ShapeDtypeStruct((batch * seq, heads * ATT_VD), BF16),
        scratch_shapes=[
            pltpu.VMEM((ATT_VD, 2 * tq), BF16),
            pltpu.VMEM((nk, ATT_VD + BF16_SUBLANES, tk), BF16),
            pltpu.VMEM((tk, 2 * tq), F32),
            pltpu.VMEM((2, tk, 2 * tq), BF16),
            pltpu.VMEM((2, 1, 2 * tq), F32),
            pltpu.VMEM((1, 2 * tq), F32),
            pltpu.VMEM((1, 2 * tq), F32),
            pltpu.VMEM((ATT_VD + BF16_SUBLANES, 2 * tq), F32),
        ],
        compiler_params=pltpu.CompilerParams(
            dimension_semantics=("arbitrary", "arbitrary", "arbitrary"),
            vmem_limit_bytes=VMEM_LIMIT),
        name="diff_attention",
    )(lamv, proj, proj, proj, g_sub)


def _out_proj_kernel(ax_ref, ab_ref, ac_ref, axp_ref, acp_ref, axn_ref, acn_ref, cw_ref,
                     att_ref, oc_ref, x_ref, w_ref, o_ref, mix_sc, *, pos_blocks):
    i = pl.program_id(0)
    j = pl.program_id(1)
    tm, cc = ax_ref.shape

    @pl.when(j == 0)
    def _():
        z = ac_ref[...].astype(F32) * ax_ref[...].astype(F32)
        last = BF16_SUBLANES - 1
        zp = acp_ref[last:last + 1, :].astype(F32) * axp_ref[last:last + 1, :].astype(F32)
        zn = acn_ref[0:1, :].astype(F32) * axn_ref[0:1, :].astype(F32)
        zp = jnp.where(i % pos_blocks == 0, jnp.zeros_like(zp), zp)
        zn = jnp.where(i % pos_blocks == pos_blocks - 1, jnp.zeros_like(zn), zn)
        row = lax.broadcasted_iota(jnp.int32, z.shape, 0)
        z_prev = jnp.where(row == 0, zp, pltpu.roll(z, 1, 0))
        z_next = jnp.where(row == tm - 1, zn, pltpu.roll(z, tm - 1, 0))
        cw = cw_ref[...]
        conv = cw[0:1] * z_prev + cw[1:2] * z + cw[2:3] * z_next
        mix_sc[:, 0:cc] = (ab_ref[...].astype(F32) * conv).astype(BF16)
        aw = att_ref.shape[1]
        mix_sc[:, cc:cc + aw] = att_ref[...]
        mix_sc[:, cc + aw:] = oc_ref[...]

    o_ref[...] = x_ref[...] + jnp.dot(mix_sc[...], w_ref[...], preferred_element_type=F32)


def _out_proj(proj, att, x, conv_w, w_out, *, seq, tm, tn, oc_col):
    t, d = x.shape
    pos_blocks = seq // tm
    hb = tm // BF16_SUBLANES
    n_hb = t // BF16_SUBLANES
    prev = lambda i: jnp.maximum(i * hb - 1, 0)
    nxt = lambda i: jnp.minimum((i + 1) * hb, n_hb - 1)
    kern = functools.partial(_out_proj_kernel, pos_blocks=pos_blocks)
    return pl.pallas_call(
        kern,
        grid=(t // tm, d // tn),
        in_specs=[
            pl.BlockSpec((tm, COL), lambda i, j: (i, 0)),
            pl.BlockSpec((tm, COL), lambda i, j: (i, 1)),
            pl.BlockSpec((tm, COL), lambda i, j: (i, 2)),
            pl.BlockSpec((BF16_SUBLANES, COL), lambda i, j: (prev(i), 0)),
            pl.BlockSpec((BF16_SUBLANES, COL), lambda i, j: (prev(i), 2)),
            pl.BlockSpec((BF16_SUBLANES, COL), lambda i, j: (nxt(i), 0)),
            pl.BlockSpec((BF16_SUBLANES, COL), lambda i, j: (nxt(i), 2)),
            pl.BlockSpec(conv_w.shape, lambda i, j: (0, 0)),
            pl.BlockSpec((tm, att.shape[1]), lambda i, j: (i, 0)),
            pl.BlockSpec((tm, COL), lambda i, j: (i, oc_col)),
            pl.BlockSpec((tm, tn), lambda i, j: (i, j)),
            pl.BlockSpec((w_out.shape[0], tn), lambda i, j: (0, j)),
        ],
        out_specs=pl.BlockSpec((tm, tn), lambda i, j: (i, j)),
        out_shape=jax.ShapeDtypeStruct((t, d), F32),
        scratch_shapes=[pltpu.VMEM((tm, w_out.shape[0]), BF16)],
        compiler_params=pltpu.CompilerParams(
            dimension_semantics=("parallel", "arbitrary"), vmem_limit_bytes=VMEM_LIMIT),
        name="out_proj",
    )(proj, proj, proj, proj, proj, proj, proj, conv_w, att, proj, x, w_out)


def _ffn_kernel(x_ref, g_ref, wu_ref, wd_ref, o_ref, h_sc):
    f = pl.program_id(1)

    @pl.when(f == 0)
    def _():
        x = x_ref[...]
        h_sc[...] = _rms(x, g_ref[...]).astype(BF16)
        o_ref[...] = x

    hid = jnp.dot(h_sc[...], wu_ref[...], preferred_element_type=F32)
    hid = jnp.square(jnp.maximum(hid, 0.0)).astype(BF16)
    o_ref[...] += jnp.dot(hid, wd_ref[...], preferred_element_type=F32)


def _ffn(x, g2, w_up, w_down, *, tm, tf):
    t, d = x.shape
    ff = w_up.shape[1]
    return pl.pallas_call(
        _ffn_kernel,
        grid=(t // tm, ff // tf),
        in_specs=[
            pl.BlockSpec((tm, d), lambda i, f: (i, 0)),
            pl.BlockSpec((1, d), lambda i, f: (0, 0)),
            pl.BlockSpec((d, tf), lambda i, f: (0, f)),
            pl.BlockSpec((tf, d), lambda i, f: (f, 0)),
        ],
        out_specs=pl.BlockSpec((tm, d), lambda i, f: (i, 0)),
        out_shape=jax.ShapeDtypeStruct((t, d), F32),
        scratch_shapes=[pltpu.VMEM((tm, d), BF16)],
        compiler_params=pltpu.CompilerParams(
            dimension_semantics=("parallel", "arbitrary"), vmem_limit_bytes=VMEM_LIMIT),
        name="ffn",
    )(x, g2, w_up, w_down)


def _rope_tables(seq, rot_dim):
    half = rot_dim // 2
    inv = ROPE_THETA ** (-jnp.arange(0, rot_dim, 2, dtype=F32) / rot_dim)
    ang = jnp.arange(seq, dtype=F32)[:, None] * inv[None, :]
    cos, sin = jnp.cos(ang), jnp.sin(ang)
    pad = ATT_HD - rot_dim
    ones = jnp.ones((seq, pad), F32)
    zeros = jnp.zeros((seq, pad), F32)
    rep = LANES // ATT_HD
    cos_t = jnp.tile(jnp.concatenate([cos, cos, ones], axis=1), (1, rep))
    sin_t = jnp.tile(jnp.concatenate([-sin, sin, zeros], axis=1), (1, rep))
    return cos_t, sin_t


def _prep_layer(l, norm1_g, w_in, conv_w, q_norm_g, k_norm_g, lam_q1, lam_k1, lam_q2, lam_k2,
                subln_g, sgu_norm_g, sgu_w, sgu_b, w_out, norm2_g, w_up, w_down):
    d = w_in.shape[1]
    mlp_heads = sgu_w.shape[1]
    grp = jnp.arange(MXU_DIM) // ATT_HD
    seg = jnp.where(grp[:, None] == grp[None, :], 1.0 / ATT_HD, 0.0).astype(BF16)
    lane = jnp.arange(MXU_DIM) % ATT_HD
    half = ATT_HD // 8
    partner = jnp.arange(MXU_DIM) + jnp.where(lane < half, half, -half)
    perm = ((jnp.arange(MXU_DIM)[:, None] == partner[None, :])
            & (lane[None, :] < 2 * half)).astype(BF16)
    return dict(
        g1=norm1_g[l].reshape(1, d),
        w_in=w_in[l].astype(BF16),
        conv_w=conv_w[l],
        gq=jnp.tile(q_norm_g[l], COL // ATT_HD).reshape(1, COL),
        gk=jnp.tile(k_norm_g[l], COL // ATT_HD).reshape(1, COL),
        seg=seg,
        perm=perm,
        lamv=jnp.stack([lam_q1[l], lam_k1[l], lam_q2[l], lam_k2[l]]).astype(F32),
        g_sub=subln_g[l].reshape(1, ATT_VD),
        gv=jnp.tile(sgu_norm_g[l], mlp_heads).reshape(1, mlp_heads * MLP_HD),
        ws=sgu_w[l].astype(BF16),
        bs=jnp.repeat(sgu_b[l].T, MLP_HD, axis=1),
        w_out=w_out[l].astype(BF16),
        g2=norm2_g[l].reshape(1, d),
        w_up=w_up[l].astype(BF16),
        w_down=w_down[l].astype(BF16),
        lambda_init=0.8 - 0.6 * math.exp(-0.3 * l),
    )


def _trunk(x, layers, tm=1024, tq=512, tk=512, tn=512, tf=512):
    b, s, d = x.shape
    t = b * s
    heads = (d // 2) // ATT_VD
    tabs = _rope_tables(s, ATT_HD // 4)
    xf = x.reshape(t, d)
    for p in layers:
        proj = _in_proj(xf, p["g1"], p["w_in"], tabs, p["gq"], p["gk"], p["seg"], p["perm"],
                        p["gv"],
                        p["ws"], p["bs"], seq=s, tm=tm)
        att = _attention(proj, p["lamv"], p["g_sub"], batch=b, seq=s, heads=heads,
                         q_col=12, k_col=12 + heads, v_col=12 + 2 * heads,
                         lambda_init=p["lambda_init"], tq=tq, tk=tk)
        x1 = _out_proj(proj, att, xf, p["conv_w"], p["w_out"], seq=s, tm=min(tm, 512), tn=d,
                       oc_col=9)
        xf = _ffn(x1, p["g2"], p["w_up"], p["w_down"], tm=tm, tf=tf)
    return xf.reshape(b, s, d)


def kernel(x_prompt, x_sample, norm1_g, w_in, conv_w, q_norm_g, k_norm_g, lam_q1, lam_k1, lam_q2,
           lam_k2, subln_g, sgu_norm_g, sgu_w, sgu_b, w_out, norm2_g, w_up, w_down):
    params = (norm1_g, w_in, conv_w, q_norm_g, k_norm_g, lam_q1, lam_k1, lam_q2, lam_k2,
              subln_g, sgu_norm_g, sgu_w, sgu_b, w_out, norm2_g, w_up, w_down)
    layers = [_prep_layer(l, *params) for l in range(w_in.shape[0])]
    return (_trunk(x_prompt, layers), _trunk(x_sample, layers))
```

```python
import functools
import math

import jax
import jax.numpy as jnp
from jax import lax
from jax.experimental import pallas as pl
from jax.experimental.pallas import tpu as pltpu

F32 = jnp.float32
BF16 = jnp.bfloat16

EPS = 1e-6
ROPE_THETA = 500000.0
ATT_HD = 64
ATT_VD = 2 * ATT_HD
KEY_CHUNK = 128
MAX_LEAD = 64.0
CHUNK = 128
MLP_HD = 128
LANES = 128
MXU_DIM = 256
BF16_SUBLANES = 16
COL = 512
VMEM_LIMIT = 56 * 1024 * 1024
LOG2E = math.log2(math.e)


def _rms(x, gain):
    ms = jnp.mean(x * x, axis=-1, keepdims=True)
    return x * lax.rsqrt(ms + EPS) * gain


def _in_proj_kernel(x_ref, g1_ref, w_ref, cos_ref, sin_ref, gq_ref, gk_ref, seg_ref, perm_ref,
                    gv_ref, ws_ref, bs_ref, o_ref, h_sc, *, n_conv, n_qk, n_v):
    tm = x_ref.shape[0]
    h_sc[...] = _rms(x_ref[...], g1_ref[...]).astype(BF16)

    q_lo, k_lo = n_conv, n_conv + n_qk
    v_lo, u_at = k_lo + n_qk, k_lo + n_qk + n_v

    def qk_norm_rope(acc, gain, out_cols):
        sq = (acc * acc).astype(BF16)
        ms = jnp.concatenate(
            [jnp.dot(sq[:, c:c + MXU_DIM], seg_ref[...], preferred_element_type=F32)
             for c in range(0, COL, MXU_DIM)], axis=1)
        y = acc * lax.rsqrt(ms + EPS) * gain
        yb = y.astype(BF16)
        cos, sin = cos_ref[...], sin_ref[...]
        for c in range(0, COL, MXU_DIM):
            rot = jnp.dot(yb[:, c:c + MXU_DIM], perm_ref[...], preferred_element_type=F32)
            for cc in range(c, c + MXU_DIM, LANES):
                r = y[:, cc:cc + LANES] * cos + rot[:, cc - c:cc - c + LANES] * sin
                o_ref[:, out_cols + cc:out_cols + cc + LANES] = r.astype(BF16)

    u = None
    for j in range(u_at + 2):
        acc = jnp.dot(h_sc[...], w_ref[:, j * COL:(j + 1) * COL],
                      preferred_element_type=F32)
        out_cols = min(j, u_at) * COL
        if j < q_lo or v_lo <= j < u_at:
            o_ref[:, out_cols:out_cols + COL] = acc.astype(BF16)
        elif j < k_lo:
            qk_norm_rope(acc, gq_ref[...] * (ATT_HD ** -0.5 * LOG2E), out_cols)
        elif j < v_lo:
            qk_norm_rope(acc, gk_ref[...], out_cols)
        elif j == u_at:
            u = jax.nn.gelu(acc)
        else:
            v = jax.nn.gelu(acc)
            for h in range(COL // MLP_HD):
                sl = slice(h * MLP_HD, (h + 1) * MLP_HD)
                vn = _rms(v[:, sl], gv_ref[:, sl]).astype(BF16)
                w = ws_ref[h]
                for n in range(tm // CHUNK):
                    rows = slice(n * CHUNK, (n + 1) * CHUNK)
                    mixed = jnp.dot(w, vn[rows, :], preferred_element_type=F32) + bs_ref[:, sl]
                    o_ref[rows, out_cols + h * MLP_HD:out_cols + (h + 1) * MLP_HD] = (
                        u[rows, sl] * mixed).astype(BF16)


def _in_proj(x, g1, w_in, tabs, gq, gk, seg, perm, gv, ws, bs, *, seq, tm):
    t, d = x.shape
    n_in = w_in.shape[1] // COL
    n_out = n_in - 1
    pos_blocks = seq // tm
    const = lambda i: (0, 0)
    conv_ch = d // 4
    att_cols = d // 2
    kern = functools.partial(_in_proj_kernel, n_conv=3 * conv_ch // COL, n_qk=att_cols // COL,
                             n_v=att_cols // COL)
    return pl.pallas_call(
        kern,
        grid=(t // tm,),
        in_specs=[
            pl.BlockSpec((tm, d), lambda i: (i, 0)),
            pl.BlockSpec((1, d), const),
            pl.BlockSpec(w_in.shape, const, pipeline_mode=pl.Buffered(1)),
            pl.BlockSpec((tm, LANES), lambda i: (i % pos_blocks, 0)),
            pl.BlockSpec((tm, LANES), lambda i: (i % pos_blocks, 0)),
            pl.BlockSpec((1, COL), const),
            pl.BlockSpec((1, COL), const),
            pl.BlockSpec((MXU_DIM, MXU_DIM), const),
            pl.BlockSpec((MXU_DIM, MXU_DIM), const),
            pl.BlockSpec((1, COL), const),
            pl.BlockSpec(ws.shape, lambda i: (0, 0, 0)),
            pl.BlockSpec((CHUNK, COL), const),
        ],
        out_specs=pl.BlockSpec((tm, n_out * COL), lambda i: (i, 0)),
        out_shape=jax.ShapeDtypeStruct((t, n_out * COL), BF16),
        scratch_shapes=[pltpu.VMEM((tm, d), BF16)],
        compiler_params=pltpu.CompilerParams(
            dimension_semantics=("arbitrary",), vmem_limit_bytes=VMEM_LIMIT),
        name="in_proj",
    )(x, g1, w_in, *tabs, gq, gk, seg, perm, gv, ws, bs)


def _attn_kernel(lamv_ref, q_ref, k_ref, v_ref, g_ref, o_ref,
                 qst_sc, vt_sc, s_sc, p_sc, al_sc, m_sc, lead_sc, acc_sc, *, lambda_init):
    i = pl.program_id(2)
    tq = q_ref.shape[0]
    nk, _, tk = vt_sc.shape
    r = 2 * tq

    @pl.when(i == 0)
    def _():
        ones = jnp.ones((BF16_SUBLANES, tk), BF16)
        for c in range(nk):
            vt_sc[c, 0:ATT_VD, :] = v_ref[c * tk:(c + 1) * tk, :].astype(F32).T.astype(BF16)
            vt_sc[c, ATT_VD:, :] = ones

    qt = q_ref[...].astype(F32).T
    row = lax.broadcasted_iota(jnp.int32, qt.shape, 0)
    zero = jnp.zeros_like(qt)
    qst_sc[:, 0:tq] = jnp.where(row < ATT_HD, qt, zero).astype(BF16)
    qst_sc[:, tq:r] = jnp.where(row >= ATT_HD, qt, zero).astype(BF16)
    n_chunks = tk // KEY_CHUNK

    def score_chunk(j, c):
        off = pl.multiple_of(j * tk + c * KEY_CHUNK, KEY_CHUNK)
        return jnp.dot(k_ref[pl.ds(off, KEY_CHUNK), :], qst_sc[...],
                       preferred_element_type=F32)

    def add_values(j, slot):
        acc_sc[...] = al_sc[slot] * (acc_sc[...] + jnp.dot(
            vt_sc[j], p_sc[slot], preferred_element_type=F32))

    m_sc[...] = jnp.max(score_chunk(0, 0), axis=0, keepdims=True)
    lead_sc[...] = jnp.zeros(lead_sc.shape, F32)
    acc_sc[...] = jnp.zeros(acc_sc.shape, F32)
    p_sc[1] = jnp.zeros(p_sc.shape[1:], BF16)
    al_sc[1] = jnp.ones(al_sc.shape[1:], F32)

    def trip(j, slot):
        m_prev = m_sc[...]
        bm = None
        for c in range(n_chunks):
            rows = slice(c * KEY_CHUNK, (c + 1) * KEY_CHUNK)
            s = score_chunk(j, c)
            p_sc[slot, rows, :] = jnp.exp2(s - m_prev).astype(BF16)
            cmx = jnp.max(s, axis=0, keepdims=True)
            bm = cmx if bm is None else jnp.maximum(bm, cmx)
            if c == 0:
                add_values(max(j - 1, 0), 1 - slot)
        m_new = jnp.maximum(m_prev, bm)
        m_sc[...] = m_new
        al_sc[slot] = jnp.exp2(m_prev - m_new)
        lead_sc[...] = jnp.maximum(lead_sc[...], bm - m_prev)

    for j in range(nk):
        trip(j, j % 2)
    add_values(nk - 1, (nk - 1) % 2)

    def write_output():
        lv = lamv_ref[...]
        lam = (jnp.exp(jnp.sum(lv[0:1] * lv[1:2], axis=1, keepdims=True))
               - jnp.exp(jnp.sum(lv[2:3] * lv[3:4], axis=1, keepdims=True)) + lambda_init)
        o_all = acc_sc[0:ATT_VD, :] / acc_sc[ATT_VD:ATT_VD + 1, :]
        ot = o_all[:, 0:tq] - lam * o_all[:, tq:r]
        ms = jnp.mean(ot * ot, axis=0, keepdims=True)
        o = (ot * lax.rsqrt(ms + EPS)).T * g_ref[...]
        o_ref[...] = (o * (1.0 - lambda_init)).astype(BF16)

    write_output()

    @pl.when(jnp.max(lead_sc[...]) > MAX_LEAD)
    def _():
        m_sc[...] = jnp.full(m_sc.shape, -jnp.inf, F32)
        acc_sc[...] = jnp.zeros(acc_sc.shape, F32)

        def block(j, carry):
            m_prev = m_sc[...]
            for c in range(n_chunks):
                s_sc[c * KEY_CHUNK:(c + 1) * KEY_CHUNK, :] = score_chunk(j, c)
            s = s_sc[...]
            m_new = jnp.maximum(m_prev, jnp.max(s, axis=0, keepdims=True))
            m_sc[...] = m_new
            acc_sc[...] = jnp.exp2(m_prev - m_new) * acc_sc[...] + jnp.dot(
                vt_sc[j], jnp.exp2(s - m_new).astype(BF16), preferred_element_type=F32)
            return carry

        lax.fori_loop(0, nk, block, 0)
        write_output()


def _attention(proj, lamv, g_sub, *, batch, seq, heads, q_col, k_col, v_col, lambda_init, tq, tk):
    nq = seq // tq
    nk = seq // tk
    assert seq % tk == 0 and seq % tq == 0 and tk % KEY_CHUNK == 0
    kern = functools.partial(_attn_kernel, lambda_init=lambda_init)
    return pl.pallas_call(
        kern,
        grid=(batch, heads, nq),
        in_specs=[
            pl.BlockSpec(lamv.shape, lambda b, h, i: (0, 0)),
            pl.BlockSpec((tq, ATT_VD), lambda b, h, i: (b * nq + i, q_col + h)),
            pl.BlockSpec((seq, ATT_VD), lambda b, h, i: (b, k_col + h)),
            pl.BlockSpec((seq, ATT_VD), lambda b, h, i: (b, v_col + h)),
            pl.BlockSpec((1, ATT_VD), lambda b, h, i: (0, 0)),
        ],
        out_specs=pl.BlockSpec((tq, ATT_VD), lambda b, h, i: (b * nq + i, h)),
        out_shape=jax.ShapeDtypeStruct((batch * seq, heads * ATT_VD), BF16),
        scratch_shapes=[
            pltpu.VMEM((ATT_VD, 2 * tq), BF16),
            pltpu.VMEM((nk, ATT_VD + BF16_SUBLANES, tk), BF16),
            pltpu.VMEM((tk, 2 * tq), F32),
            pltpu.VMEM((2, tk, 2 * tq), BF16),
            pltpu.VMEM((2, 1, 2 * tq), F32),
            pltpu.VMEM((1, 2 * tq), F32),
            pltpu.VMEM((1, 2 * tq), F32),
            pltpu.VMEM((ATT_VD + BF16_SUBLANES, 2 * tq), F32),
        ],
        compiler_params=pltpu.CompilerParams(
            dimension_semantics=("arbitrary", "arbitrary", "arbitrary"),
            vmem_limit_bytes=VMEM_LIMIT),
        name="diff_attention",
    )(lamv, proj, proj, proj, g_sub)


def _out_proj_kernel(ax_ref, ab_ref, ac_ref, axp_ref, acp_ref, axn_ref, acn_ref, cw_ref,
                     att_ref, oc_ref, x_ref, w_ref, o_ref, mix_sc, *, pos_blocks):
    i = pl.program_id(0)
    j = pl.program_id(1)
    tm, cc = ax_ref.shape

    @pl.when(j == 0)
    def _():
        z = ac_ref[...].astype(F32) * ax_ref[...].astype(F32)
        last = BF16_SUBLANES - 1
        zp = acp_ref[last:last + 1, :].astype(F32) * axp_ref[last:last + 1, :].astype(F32)
        zn = acn_ref[0:1, :].astype(F32) * axn_ref[0:1, :].astype(F32)
        zp = jnp.where(i % pos_blocks == 0, jnp.zeros_like(zp), zp)
        zn = jnp.where(i % pos_blocks == pos_blocks - 1, jnp.zeros_like(zn), zn)
        row = lax.broadcasted_iota(jnp.int32, z.shape, 0)
        z_prev = jnp.where(row == 0, zp, pltpu.roll(z, 1, 0))
        z_next = jnp.where(row == tm - 1, zn, pltpu.roll(z, tm - 1, 0))
        cw = cw_ref[...]
        conv = cw[0:1] * z_prev + cw[1:2] * z + cw[2:3] * z_next
        mix_sc[:, 0:cc] = (ab_ref[...].astype(F32) * conv).astype(BF16)
        aw = att_ref.shape[1]
        mix_sc[:, cc:cc + aw] = att_ref[...]
        mix_sc[:, cc + aw:] = oc_ref[...]

    o_ref[...] = x_ref[...] + jnp.dot(mix_sc[...], w_ref[...], preferred_element_type=F32)


def _out_proj(proj, att, x, conv_w, w_out, *, seq, tm, tn, oc_col):
    t, d = x.shape
    pos_blocks = seq // tm
    hb = tm // BF16_SUBLANES
    n_hb = t // BF16_SUBLANES
    prev = lambda i: jnp.maximum(i * hb - 1, 0)
    nxt = lambda i: jnp.minimum((i + 1) * hb, n_hb - 1)
    kern = functools.partial(_out_proj_kernel, pos_blocks=pos_blocks)
    return pl.pallas_call(
        kern,
        grid=(t // tm, d // tn),
        in_specs=[
            pl.BlockSpec((tm, COL), lambda i, j: (i, 0)),
            pl.BlockSpec((tm, COL), lambda i, j: (i, 1)),
            pl.BlockSpec((tm, COL), lambda i, j: (i, 2)),
            pl.BlockSpec((BF16_SUBLANES, COL), lambda i, j: (prev(i), 0)),
            pl.BlockSpec((BF16_SUBLANES, COL), lambda i, j: (prev(i), 2)),
            pl.BlockSpec((BF16_SUBLANES, COL), lambda i, j: (nxt(i), 0)),
            pl.BlockSpec((BF16_SUBLANES, COL), lambda i, j: (nxt(i), 2)),
            pl.BlockSpec(conv_w.shape, lambda i, j: (0, 0)),
            pl.BlockSpec((tm, att.shape[1]), lambda i, j: (i, 0)),
            pl.BlockSpec((tm, COL), lambda i, j: (i, oc_col)),
            pl.BlockSpec((tm, tn), lambda i, j: (i, j)),
            pl.BlockSpec((w_out.shape[0], tn), lambda i, j: (0, j)),
        ],
        out_specs=pl.BlockSpec((tm, tn), lambda i, j: (i, j)),
        out_shape=jax.ShapeDtypeStruct((t, d), F32),
        scratch_shapes=[pltpu.VMEM((tm, w_out.shape[0]), BF16)],
        compiler_params=pltpu.CompilerParams(
            dimension_semantics=("parallel", "arbitrary"), vmem_limit_bytes=VMEM_LIMIT),
        name="out_proj",
    )(proj, proj, proj, proj, proj, proj, proj, conv_w, att, proj, x, w_out)


def _ffn_kernel(x_ref, g_ref, wu_ref, wd_ref, o_ref, h_sc):
    f = pl.program_id(1)

    @pl.when(f == 0)
    def _():
        x = x_ref[...]
        h_sc[...] = _rms(x, g_ref[...]).astype(BF16)
        o_ref[...] = x

    hid = jnp.dot(h_sc[...], wu_ref[...], preferred_element_type=F32)
    hid = jnp.square(jnp.maximum(hid, 0.0)).astype(BF16)
    o_ref[...] += jnp.dot(hid, wd_ref[...], preferred_element_type=F32)


def _ffn(x, g2, w_up, w_down, *, tm, tf):
    t, d = x.shape
    ff = w_up.shape[1]
    return pl.pallas_call(
        _ffn_kernel,
        grid=(t // tm, ff // tf),
        in_specs=[
            pl.BlockSpec((tm, d), lambda i, f: (i, 0)),
            pl.BlockSpec((1, d), lambda i, f: (0, 0)),
            pl.BlockSpec((d, tf), lambda i, f: (0, f)),
            pl.BlockSpec((tf, d), lambda i, f: (f, 0)),
        ],
        out_specs=pl.BlockSpec((tm, d), lambda i, f: (i, 0)),
        out_shape=jax.ShapeDtypeStruct((t, d), F32),
        scratch_shapes=[pltpu.VMEM((tm, d), BF16)],
        compiler_params=pltpu.CompilerParams(
            dimension_semantics=("parallel", "arbitrary"), vmem_limit_bytes=VMEM_LIMIT),
        name="ffn",
    )(x, g2, w_up, w_down)


def _rope_tables(seq, rot_dim):
    half = rot_dim // 2
    inv = ROPE_THETA ** (-jnp.arange(0, rot_dim, 2, dtype=F32) / rot_dim)
    ang = jnp.arange(seq, dtype=F32)[:, None] * inv[None, :]
    cos, sin = jnp.cos(ang), jnp.sin(ang)
    pad = ATT_HD - rot_dim
    ones = jnp.ones((seq, pad), F32)
    zeros = jnp.zeros((seq, pad), F32)
    rep = LANES // ATT_HD
    cos_t = jnp.tile(jnp.concatenate([cos, cos, ones], axis=1), (1, rep))
    sin_t = jnp.tile(jnp.concatenate([-sin, sin, zeros], axis=1), (1, rep))
    return cos_t, sin_t


def _prep_layer(l, norm1_g, w_in, conv_w, q_norm_g, k_norm_g, lam_q1, lam_k1, lam_q2, lam_k2,
                subln_g, sgu_norm_g, sgu_w, sgu_b, w_out, norm2_g, w_up, w_down):
    d = w_in.shape[1]
    mlp_heads = sgu_w.shape[1]
    grp = jnp.arange(MXU_DIM) // ATT_HD
    seg = jnp.where(grp[:, None] == grp[None, :], 1.0 / ATT_HD, 0.0).astype(BF16)
    lane = jnp.arange(MXU_DIM) % ATT_HD
    half = ATT_HD // 8
    partner = jnp.arange(MXU_DIM) + jnp.where(lane < half, half, -half)
    perm = ((jnp.arange(MXU_DIM)[:, None] == partner[None, :])
            & (lane[None, :] < 2 * half)).astype(BF16)
    return dict(
        g1=norm1_g[l].reshape(1, d),
        w_in=w_in[l].astype(BF16),
        conv_w=conv_w[l],
        gq=jnp.tile(q_norm_g[l], COL // ATT_HD).reshape(1, COL),
        gk=jnp.tile(k_norm_g[l], COL // ATT_HD).reshape(1, COL),
        seg=seg,
        perm=perm,
        lamv=jnp.stack([lam_q1[l], lam_k1[l], lam_q2[l], lam_k2[l]]).astype(F32),
        g_sub=subln_g[l].reshape(1, ATT_VD),
        gv=jnp.tile(sgu_norm_g[l], mlp_heads).reshape(1, mlp_heads * MLP_HD),
        ws=sgu_w[l].astype(BF16),
        bs=jnp.repeat(sgu_b[l].T, MLP_HD, axis=1),
        w_out=w_out[l].astype(BF16),
        g2=norm2_g[l].reshape(1, d),
        w_up=w_up[l].astype(BF16),
        w_down=w_down[l].astype(BF16),
        lambda_init=0.8 - 0.6 * math.exp(-0.3 * l),
    )


def _trunk(x, layers, tm=1024, tq=512, tk=512, tn=512, tf=512):
    b, s, d = x.shape
    t = b * s
    heads = (d // 2) // ATT_VD
    tabs = _rope_tables(s, ATT_HD // 4)
    xf = x.reshape(t, d)
    for p in layers:
        proj = _in_proj(xf, p["g1"], p["w_in"], tabs, p["gq"], p["gk"], p["seg"], p["perm"],
                        p["gv"],
                        p["ws"], p["bs"], seq=s, tm=min(tm, 512))
        q_col = 3 * (d // 4) // ATT_VD
        att = _attention(proj, p["lamv"], p["g_sub"], batch=b, seq=s, heads=heads,
                         q_col=q_col, k_col=q_col + heads, v_col=q_col + 2 * heads,
                         lambda_init=p["lambda_init"], tq=tq, tk=tk)
        x1 = _out_proj(proj, att, xf, p["conv_w"], p["w_out"], seq=s, tm=min(tm, 512), tn=d,
                       oc_col=proj.shape[1] // COL - 1)
        xf = _ffn(x1, p["g2"], p["w_up"], p["w_down"], tm=tm, tf=tf)
    return xf.reshape(b, s, d)


def kernel(x_prompt, x_sample, norm1_g, w_in, conv_w, q_norm_g, k_norm_g, lam_q1, lam_k1, lam_q2,
           lam_k2, subln_g, sgu_norm_g, sgu_w, sgu_b, w_out, norm2_g, w_up, w_down):
    params = (norm1_g, w_in, conv_w, q_norm_g, k_norm_g, lam_q1, lam_k1, lam_q2, lam_k2,
              subln_g, sgu_norm_g, sgu_w, sgu_b, w_out, norm2_g, w_up, w_down)
    layers = [_prep_layer(l, *params) for l in range(w_in.shape[0])]
    return (_trunk(x_prompt, layers), _trunk(x_sample, layers))
```
